```python
import math
import jax, jax.numpy as jnp
from jax import lax
import numpy as np

D_MODEL = 1024
BATCH = 16
SEQ = 2048
DEPTH = 1
DEC_BATCH = 8
DEC_SEQ = 4096
PAST_LEN = 128

HEAD_DIM = 64
GRID_W = 64
Q_BLOCK = 128
ROPE_THETA = 10000.0
EPS = 1e-6
A_HEADS = D_MODEL // (2 * HEAD_DIM)
A_KV_HEADS = max(1, A_HEADS // 4)
A_WIDTH = A_HEADS * HEAD_DIM
A_KV = A_KV_HEADS * HEAD_DIM
B_HEADS = D_MODEL // (4 * HEAD_DIM)
B_QK = B_HEADS * 2 * HEAD_DIM
B_WIDTH = B_HEADS * 2 * HEAD_DIM
MIX_WIDTH = A_WIDTH + B_WIDTH
IN_SIZES = (A_WIDTH, A_KV, A_KV, A_WIDTH, B_QK, B_QK, B_WIDTH, B_WIDTH)
IN_WIDTH = sum(IN_SIZES)
IN_SPLITS = tuple(int(v) for v in np.cumsum(IN_SIZES)[:-1])

kernel_name = "hybrid_gqa_axial_diffattn_encoder"


def rmsnorm(x, g):
    xf = x.astype(jnp.float32)
    y = xf * lax.rsqrt(jnp.mean(xf * xf, axis=-1, keepdims=True) + EPS)
    return (y * g.astype(jnp.float32)).astype(x.dtype)


def rope(x, pos):
    dim = x.shape[-1]
    inv = ROPE_THETA ** (-jnp.arange(0, dim, 2, dtype=jnp.float32) / dim)
    ang = pos[:, None] * inv[None, :]
    shape = (ang.shape[0],) + (1,) * (x.ndim - 3) + (dim // 2,)
    c = jnp.cos(ang).reshape(shape)
    s = jnp.sin(ang).reshape(shape)
    xf = x.astype(jnp.float32)
    x1, x2 = xf[..., : dim // 2], xf[..., dim // 2:]
    return jnp.concatenate([x1 * c - x2 * s, x2 * c + x1 * s], axis=-1).astype(x.dtype)


def axial_rope(x, row, col):
    half = x.shape[-1] // 2
    return jnp.concatenate([rope(x[..., :half], row), rope(x[..., half:], col)], axis=-1)


def query_blocks(fn, q):
    B, S = q.shape[0], q.shape[1]
    nb = S // Q_BLOCK
    qb = jnp.swapaxes(q.reshape((B, nb, Q_BLOCK) + q.shape[2:]), 0, 1)
    out = lax.map(fn, qb)
    out = jnp.swapaxes(out, 0, 1)
    return out.reshape((B, S) + out.shape[3:])


def gqa_attention(q, k, v):
    B = q.shape[0]
    G = A_HEADS // A_KV_HEADS
    scale = HEAD_DIM ** -0.5

    def blk(qb):
        qb = qb.reshape(B, Q_BLOCK, A_KV_HEADS, G, HEAD_DIM)
        s = jnp.einsum('bqhgd,bkhd->bhgqk', qb, k, preferred_element_type=jnp.float32) * scale
        p = jax.nn.softmax(s, axis=-1).astype(v.dtype)
        o = jnp.einsum('bhgqk,bkhd->bqhgd', p, v)
        return o.reshape(B, Q_BLOCK, A_HEADS, HEAD_DIM)

    return query_blocks(blk, q)


def diff_attention(q, k, v, lam):
    scale = HEAD_DIM ** -0.5

    def blk(qb):
        s = jnp.einsum('bqhmd,bkhmd->bhmqk', qb, k, preferred_element_type=jnp.float32) * scale
        p = jax.nn.softmax(s, axis=-1)
        w = (p[:, :, 0] - lam * p[:, :, 1]).astype(v.dtype)
        return jnp.einsum('bhqk,bkhe->bqhe', w, v)

    return query_blocks(blk, q)


def mixer_layer(x, l, g_norm, w_in, a_q_norm, a_k_norm, b_lambda_q1, b_lambda_k1,
                b_lambda_q2, b_lambda_k2, b_subln, w_out):
    Bsz, S, _ = x.shape
    rows = S // GRID_W
    row = jnp.repeat(jnp.arange(rows, dtype=jnp.float32), GRID_W)
    col = jnp.tile(jnp.arange(GRID_W, dtype=jnp.float32), rows)
    pos = jnp.arange(S, dtype=jnp.float32)

    h = rmsnorm(x, g_norm[l])
    proj = jnp.einsum('bsd,de->bse', h, w_in[l])
    qa, ka, va, ga, qb, kb, vb, gb = jnp.split(proj, IN_SPLITS, axis=-1)

    qa = rmsnorm(qa.reshape(Bsz, S, A_HEADS, HEAD_DIM), a_q_norm[l])
    ka = rmsnorm(ka.reshape(Bsz, S, A_KV_HEADS, HEAD_DIM), a_k_norm[l])
    va = va.reshape(Bsz, S, A_KV_HEADS, HEAD_DIM)
    qa = axial_rope(qa, row, col)
    ka = axial_rope(ka, row, col)
    oa = gqa_attention(qa, ka, va).reshape(Bsz, S, A_WIDTH) * jax.nn.silu(ga)

    lam_init = 0.8 - 0.6 * math.exp(-0.3 * l)
    f32 = jnp.float32
    lam = (jnp.exp(jnp.sum(b_lambda_q1[l].astype(f32) * b_lambda_k1[l].astype(f32)))
           - jnp.exp(jnp.sum(b_lambda_q2[l].astype(f32) * b_lambda_k2[l].astype(f32)))
           + lam_init)
    qb = rope(qb.reshape(Bsz, S, B_HEADS, 2, HEAD_DIM), pos)
    kb = rope(kb.reshape(Bsz, S, B_HEADS, 2, HEAD_DIM), pos)
    vb = vb.reshape(Bsz, S, B_HEADS, 2 * HEAD_DIM)
    ob = diff_attention(qb, kb, vb, lam)
    ob = rmsnorm(ob, b_subln[l]) * (1.0 - lam_init)
    ob = ob.reshape(Bsz, S, B_WIDTH) * jax.nn.silu(gb)

    mixed = jnp.concatenate([oa, ob], axis=-1)
    return x + jnp.einsum('bse,ed->bsd', mixed, w_out[l])


def setup_inputs(seed: int = 0) -> dict:
    key = jax.random.key(seed)
    ks = jax.random.split(key, 14)
    f = jnp.float32
    nrm = lambda k, shape, s: jax.random.normal(k, shape, f) * s
    return {
        "x_prompt": nrm(ks[0], (BATCH, SEQ, D_MODEL), 1.0),
        "x_sample": nrm(ks[1], (DEC_BATCH, DEC_SEQ, D_MODEL), 1.0),
        "g_norm": 1.0 + nrm(ks[2], (DEPTH, D_MODEL), 0.02),
        "w_in": nrm(ks[3], (DEPTH, D_MODEL, IN_WIDTH), D_MODEL ** -0.5),
        "a_q_norm": 1.0 + nrm(ks[4], (DEPTH, HEAD_DIM), 0.02),
        "a_k_norm": 1.0 + nrm(ks[5], (DEPTH, HEAD_DIM), 0.02),
        "b_lambda_q1": nrm(ks[6], (DEPTH, HEAD_DIM), 0.1),
        "b_lambda_k1": nrm(ks[7], (DEPTH, HEAD_DIM), 0.1),
        "b_lambda_q2": nrm(ks[8], (DEPTH, HEAD_DIM), 0.1),
        "b_lambda_k2": nrm(ks[9], (DEPTH, HEAD_DIM), 0.1),
        "b_subln": 1.0 + nrm(ks[10], (DEPTH, 2 * HEAD_DIM), 0.02),
        "w_out": nrm(ks[11], (DEPTH, MIX_WIDTH, D_MODEL), MIX_WIDTH ** -0.5),
        "g_final": 1.0 + nrm(ks[12], (D_MODEL,), 0.02),
    }


def reference(x_prompt, x_sample, g_norm, w_in, a_q_norm, a_k_norm, b_lambda_q1, b_lambda_k1,
              b_lambda_q2, b_lambda_k2, b_subln, w_out, g_final):
    hp = x_prompt
    hs = x_sample
    for l in range(DEPTH):
        hp = mixer_layer(hp, l, g_norm, w_in, a_q_norm, a_k_norm, b_lambda_q1, b_lambda_k1,
                         b_lambda_q2, b_lambda_k2, b_subln, w_out)
        hs = mixer_layer(hs, l, g_norm, w_in, a_q_norm, a_k_norm, b_lambda_q1, b_lambda_k1,
                         b_lambda_q2, b_lambda_k2, b_subln, w_out)
    y_prompt = rmsnorm(hp, g_final)
    y_sample = rmsnorm(hs, g_final)
    return (y_prompt, y_sample)
```

```python
import functools
import math

import jax
import jax.numpy as jnp
from jax import lax
from jax.experimental import pallas as pl
from jax.experimental.pallas import tpu as pltpu

HEAD_DIM = 64
GRID_W = 64
ROPE_THETA = 10000.0
EPS = 1e-6
TOKEN_TILE = 512
UNIT_ROWS = 2 * HEAD_DIM
VMEM_LIMIT_BYTES = 56 * 1024 * 1024

F32 = jnp.float32
BF16 = jnp.bfloat16


def _dot_nt(a, b):
    return lax.dot_general(a, b, (((1,), (1,)), ((), ())), preferred_element_type=F32)


def _dot_tn(a, b):
    return lax.dot_general(a, b, (((0,), (0,)), ((), ())), preferred_element_type=F32)


def _dot(a, b):
    return jnp.dot(a, b, preferred_element_type=F32)


def _rope_axial(x, c, s):
    xs = jnp.concatenate([x[16:32], x[0:16], x[48:64], x[32:48]], axis=0)
    return x * c + xs * s


def _rope_1d(x, c, s):
    xs = jnp.concatenate([x[32:64], x[0:32]], axis=0)
    return x * c + xs * s


def _silu(g):
    return g * (1.0 / (1.0 + jnp.exp(-g)))


def _proj_kernel(x_ref, gn_ref, wt_ref, qn_ref, kn_ref, ca_ref, sa_ref, cb_ref, sb_ref,
                 qa_ref, va_ref, ga_ref, qb_ref, vb_ref, gb_ref, ka_ref, kb_ref, *, sizes):
    a_width, a_kv, b_qk, b_width = sizes
    x = x_ref[0]
    h = x * lax.rsqrt(jnp.mean(x * x, axis=-1, keepdims=True) + EPS) * gn_ref[...]
    hb = h.astype(BF16)

    def section(lo, n):
        return _dot_nt(wt_ref[lo:lo + n, :], hb)

    ca, sa, cb, sb = ca_ref[...], sa_ref[...], cb_ref[...], sb_ref[...]
    scale = HEAD_DIM ** -0.5

    def head_norm(blk, g):
        return blk * lax.rsqrt(jnp.mean(blk * blk, axis=0, keepdims=True) + EPS) * g

    off = 0
    r = section(off, a_width)
    for hd in range(a_width // HEAD_DIM):
        blk = r[hd * HEAD_DIM:(hd + 1) * HEAD_DIM]
        y = _rope_axial(head_norm(blk, qn_ref[...]), ca, sa) * scale
        qa_ref[0, 0, hd * HEAD_DIM:(hd + 1) * HEAD_DIM, :] = y.astype(BF16)
    off += a_width

    r = section(off, 2 * a_kv)
    ks = []
    for hd in range(a_kv // HEAD_DIM):
        blk = r[hd * HEAD_DIM:(hd + 1) * HEAD_DIM]
        ks.append(_rope_axial(head_norm(blk, kn_ref[...]), ca, sa))
    ka_ref[0] = jnp.concatenate(ks, axis=0).T.astype(BF16)
    va_ref[0, 0] = r[a_kv:2 * a_kv].astype(BF16)
    off += 2 * a_kv

    ga_ref[0, 0] = _silu(section(off, a_width)).astype(BF16)
    off += a_width

    r = section(off, b_qk)
    for hd in range(b_qk // HEAD_DIM):
        blk = r[hd * HEAD_DIM:(hd + 1) * HEAD_DIM]
        qb_ref[0, 0, hd * HEAD_DIM:(hd + 1) * HEAD_DIM, :] = (_rope_1d(blk, cb, sb) * scale).astype(BF16)
    off += b_qk

    r = section(off, b_qk)
    ks = [_rope_1d(r[hd * HEAD_DIM:(hd + 1) * HEAD_DIM], cb, sb) for hd in range(b_qk // HEAD_DIM)]
    kb_ref[0] = jnp.concatenate(ks, axis=0).T.astype(BF16)
    off += b_qk

    vb_ref[0, 0] = section(off, b_width).astype(BF16)
    off += b_width
    gb_ref[0, 0] = _silu(section(off, b_width)).astype(BF16)


def _attn_kernel(*refs, mode, n_chunks, lam_init):
    if mode == "A":
        q_ref, k_ref, v_ref, g_ref, o_ref, m_ref, l_ref, acc_ref = refs
    else:
        q_ref, k_ref, v_ref, g_ref, lam_ref, sub_ref, o_ref, m_ref, l_ref, acc_ref = refs
    tq = q_ref.shape[-1]
    q = q_ref[0, 0]
    z = jnp.zeros((HEAD_DIM, tq), BF16)
    lo0 = jnp.concatenate([q[0:HEAD_DIM], z], axis=0)
    hi1 = jnp.concatenate([z, q[HEAD_DIM:]], axis=0)
    if mode == "A":
        first = (pl.program_id(1) // 2) == 0
        hi0 = jnp.concatenate([z, q[0:HEAD_DIM]], axis=0)
        lo1 = jnp.concatenate([q[HEAD_DIM:], z], axis=0)
        qx = (jnp.where(first, lo0, hi0), jnp.where(first, lo1, hi1))
    else:
        qx = (lo0, hi1)

    m_ref[...] = jnp.full(m_ref.shape, -jnp.inf, F32)
    l_ref[...] = jnp.zeros(l_ref.shape, F32)
    acc_ref[...] = jnp.zeros(acc_ref.shape, F32)
    tk = v_ref.shape[-1]

    def chunk(c, carry):
        kc = k_ref[0, pl.ds(pl.multiple_of(c * tk, tk), tk), :]
        vc = v_ref[0, c]
        for i in range(2):
            s = _dot(kc, qx[i])
            m_old = m_ref[i]
            m_new = jnp.maximum(m_old, jnp.max(s, axis=0, keepdims=True))
            alpha = jnp.exp(m_old - m_new)
            p = jnp.exp(s - m_new)
            l_ref[i] = alpha * l_ref[i] + jnp.sum(p, axis=0, keepdims=True)
            acc_ref[i] = alpha * acc_ref[i] + _dot(vc, p.astype(BF16))
            m_ref[i] = m_new
        return carry

    lax.fori_loop(0, n_chunks, chunk, 0)

    g = g_ref[0, 0].astype(F32)
    if mode == "A":
        for i in range(2):
            o = acc_ref[i] * (1.0 / l_ref[i])
            rows = slice(i * HEAD_DIM, (i + 1) * HEAD_DIM)
            o_ref[0, 0, rows, :] = (o * g[rows]).astype(BF16)
    else:
        lv = lam_ref[...]
        lam = (jnp.exp(jnp.sum(lv[0:1] * lv[1:2], axis=-1, keepdims=True))
               - jnp.exp(jnp.sum(lv[2:3] * lv[3:4], axis=-1, keepdims=True)) + lam_init)
        o = acc_ref[0] * (1.0 / l_ref[0]) - lam * (acc_ref[1] * (1.0 / l_ref[1]))
        y = o * lax.rsqrt(jnp.mean(o * o, axis=0, keepdims=True) + EPS) * sub_ref[...]
        y = y * (1.0 - lam_init)
        o_ref[0, 0] = (y * g).astype(BF16)


def _out_kernel(x_ref, oa_ref, ob_ref, w_ref, gf_ref, y_ref, *, a_width, final):
    y = x_ref[0] + _dot_tn(oa_ref[0, 0], w_ref[0:a_width, :]) + _dot_tn(ob_ref[0, 0], w_ref[a_width:, :])
    if final:
        y = y * lax.rsqrt(jnp.mean(y * y, axis=-1, keepdims=True) + EPS) * gf_ref[...]
    y_ref[0] = y


def _rope_tables(seq):
    def cs(pos, dim):
        inv = ROPE_THETA ** (-jnp.arange(0, dim, 2, dtype=F32) / dim)
        ang = pos[None, :] * inv[:, None]
        return jnp.cos(ang), jnp.sin(ang)

    rows = seq // GRID_W
    row = jnp.repeat(jnp.arange(rows, dtype=F32), GRID_W)
    col = jnp.tile(jnp.arange(GRID_W, dtype=F32), rows)
    pos = jnp.arange(seq, dtype=F32)
    cr, sr = cs(row, HEAD_DIM // 2)
    cc, sc = cs(col, HEAD_DIM // 2)
    cp, sp = cs(pos, HEAD_DIM)
    ca = jnp.concatenate([cr, cr, cc, cc], axis=0)
    sa = jnp.concatenate([-sr, sr, -sc, sc], axis=0)
    cb = jnp.concatenate([cp, cp], axis=0)
    sb = jnp.concatenate([-sp, sp], axis=0)
    return ca, sa, cb, sb


def _layer(x, gn, wt, qn, kn, lam_vecs, subln, w_out, g_final, lam_init, final):
    bsz, seq, d = x.shape
    a_width = (d // (2 * HEAD_DIM)) * HEAD_DIM
    a_kv = max(1, (a_width // HEAD_DIM) // 4) * HEAD_DIM
    b_qk = (d // (4 * HEAD_DIM)) * 2 * HEAD_DIM
    b_width = b_qk
    assert a_kv == UNIT_ROWS and a_width // HEAD_DIM == 4 * (a_kv // HEAD_DIM)
    t = TOKEN_TILE
    assert seq % t == 0 and seq % GRID_W == 0
    nc = seq // t
    ca, sa, cb, sb = _rope_tables(seq)
    cparams = functools.partial(pltpu.CompilerParams, vmem_limit_bytes=VMEM_LIMIT_BYTES)

    def fm(rows):
        return jax.ShapeDtypeStruct((bsz, nc, rows, t), BF16)

    fm_spec = lambda rows: pl.BlockSpec((1, 1, rows, t), lambda b, i: (b, i, 0, 0))
    tab_spec = pl.BlockSpec((HEAD_DIM, t), lambda b, i: (0, i))
    full = lambda shape: pl.BlockSpec(shape, lambda b, i: (0,) * len(shape))

    qa, va, ga, qb, vb, gb, ka, kb = pl.pallas_call(
        functools.partial(_proj_kernel, sizes=(a_width, a_kv, b_qk, b_width)),
        grid=(bsz, nc),
        in_specs=[pl.BlockSpec((1, t, d), lambda b, i: (b, i, 0)), full((1, d)), full(wt.shape),
                  full((HEAD_DIM, 1)), full((HEAD_DIM, 1)), tab_spec, tab_spec, tab_spec, tab_spec],
        out_specs=[fm_spec(a_width), fm_spec(a_kv), fm_spec(a_width), fm_spec(b_qk), fm_spec(b_width),
                   fm_spec(b_width),
                   pl.BlockSpec((1, t, a_kv), lambda b, i: (b, i, 0)),
                   pl.BlockSpec((1, t, b_qk), lambda b, i: (b, i, 0))],
        out_shape=[fm(a_width), fm(a_kv), fm(a_width), fm(b_qk), fm(b_width), fm(b_width),
                   jax.ShapeDtypeStruct((bsz, seq, a_kv), BF16),
                   jax.ShapeDtypeStruct((bsz, seq, b_qk), BF16)],
        compiler_params=cparams(dimension_semantics=("parallel", "parallel")),
        name="proj",
    )(x, gn, wt, qn, kn, ca, sa, cb, sb)

    unit_spec = pl.BlockSpec((1, 1, UNIT_ROWS, t), lambda b, u, i: (b, i, u, 0))
    scratch = lambda dv: [pltpu.VMEM((2, 1, t), F32), pltpu.VMEM((2, 1, t), F32), pltpu.VMEM((2, dv, t), F32)]
    sem = ("parallel", "parallel", "arbitrary")

    oa = pl.pallas_call(
        functools.partial(_attn_kernel, mode="A", n_chunks=nc, lam_init=lam_init),
        grid=(bsz, a_width // UNIT_ROWS, nc),
        in_specs=[unit_spec,
                  pl.BlockSpec((1, seq, a_kv), lambda b, u, i: (b, 0, 0)),
                  pl.BlockSpec((1, nc, HEAD_DIM, t), lambda b, u, i: (b, 0, u // 2, 0)),
                  unit_spec],
        out_specs=unit_spec,
        out_shape=fm(a_width),
        scratch_shapes=scratch(HEAD_DIM),
        compiler_params=cparams(dimension_semantics=sem),
        name="attn_a",
    )(qa, ka, va, ga)

    ob = pl.pallas_call(
        functools.partial(_attn_kernel, mode="B", n_chunks=nc, lam_init=lam_init),
        grid=(bsz, b_width // UNIT_ROWS, nc),
        in_specs=[unit_spec,
                  pl.BlockSpec((1, seq, UNIT_ROWS), lambda b, u, i: (b, 0, u)),
                  pl.BlockSpec((1, nc, UNIT_ROWS, t), lambda b, u, i: (b, 0, u, 0)),
                  unit_spec,
                  pl.BlockSpec((4, HEAD_DIM), lambda b, u, i: (0, 0)),
                  pl.BlockSpec((UNIT_ROWS, 1), lambda b, u, i: (0, 0))],
        out_specs=unit_spec,
        out_shape=fm(b_width),
        scratch_shapes=scratch(UNIT_ROWS),
        compiler_params=cparams(dimension_semantics=sem),
        name="attn_b",
    )(qb, kb, vb, gb, lam_vecs, subln)

    return pl.pallas_call(
        functools.partial(_out_kernel, a_width=a_width, final=final),
        grid=(bsz, nc),
        in_specs=[pl.BlockSpec((1, t, d), lambda b, i: (b, i, 0)), fm_spec(a_width), fm_spec(b_width),
                  full(w_out.shape), full((1, d))],
        out_specs=pl.BlockSpec((1, t, d), lambda b, i: (b, i, 0)),
        out_shape=jax.ShapeDtypeStruct(x.shape, x.dtype),
        compiler_params=cparams(dimension_semantics=("parallel", "parallel")),
        name="out_proj",
    )(x, oa, ob, w_out, g_final)


def kernel(x_prompt, x_sample, g_norm, w_in, a_q_norm, a_k_norm, b_lambda_q1, b_lambda_k1,
           b_lambda_q2, b_lambda_k2, b_subln, w_out, g_final):
    depth = w_in.shape[0]
    hp, hs = x_prompt, x_sample
    for l in range(depth):
        lam_init = 0.8 - 0.6 * math.exp(-0.3 * l)
        args = (g_norm[l][None, :], w_in[l].T.astype(BF16), a_q_norm[l][:, None], a_k_norm[l][:, None],
                jnp.stack([b_lambda_q1[l], b_lambda_k1[l], b_lambda_q2[l], b_lambda_k2[l]]).astype(F32),
                b_subln[l][:, None], w_out[l].astype(BF16), g_final[None, :])
        final = l == depth - 1
        hp = _layer(hp, *args, lam_init, final)
        hs = _layer(hs, *args, lam_init, final)
    return (hp, hs)
```

```python
import functools
import math

import jax
import jax.numpy as jnp
from jax import lax
from jax.experimental import pallas as pl
from jax.experimental.pallas import tpu as pltpu

HEAD_DIM = 64
GRID_W = 64
ROPE_THETA = 10000.0
EPS = 1e-6
TOKEN_TILE = 512
SUB_KEYS = 256
SCORE_MAP_ELEMS = 4096 * 512
MAPS_PER_STEP = 4
STEP_ROWS = MAPS_PER_STEP * HEAD_DIM
BF16_SUBLANES = 16
VMEM_LIMIT_BYTES = 56 * 1024 * 1024
LOG2E = math.log2(math.e)

F32 = jnp.float32
BF16 = jnp.bfloat16


def _dot_nt(a, b):
    return lax.dot_general(a, b, (((1,), (1,)), ((), ())), preferred_element_type=F32)


def _dot_tn(a, b):
    return lax.dot_general(a, b, (((0,), (0,)), ((), ())), preferred_element_type=F32)


def _dot(a, b):
    return jnp.dot(a, b, preferred_element_type=F32)


def _rope_axial(x, c, s):
    xs = jnp.concatenate([x[16:32], x[0:16], x[48:64], x[32:48]], axis=0)
    return x * c + xs * s


def _rope_1d(x, c, s):
    xs = jnp.concatenate([x[32:64], x[0:32]], axis=0)
    return x * c + xs * s


def _silu(g):
    return g * (1.0 / (1.0 + jnp.exp(-g)))


def _proj_kernel(x_ref, gn_ref, wt_ref, qn_ref, kn_ref, ca_ref, sa_ref, cb_ref, sb_ref,
                 qa_ref, va_ref, ga_ref, qb_ref, vb_ref, gb_ref, ka_ref, kb_ref, *, sizes):
    a_width, a_kv, b_qk, b_width = sizes
    x = x_ref[0]
    t = x.shape[0]
    h = x * lax.rsqrt(jnp.mean(x * x, axis=-1, keepdims=True) + EPS) * gn_ref[...]
    hb = h.astype(BF16)

    def section(lo, n):
        return _dot_nt(wt_ref[lo:lo + n, :], hb)

    ca, sa, cb, sb = ca_ref[...], sa_ref[...], cb_ref[...], sb_ref[...]
    qscale = HEAD_DIM ** -0.5 * LOG2E
    ones_rows = (lax.broadcasted_iota(jnp.int32, (BF16_SUBLANES, t), 0) == 0).astype(BF16)

    def head_norm(blk, g):
        return blk * lax.rsqrt(jnp.mean(blk * blk, axis=0, keepdims=True) + EPS) * g

    def store_values(v_ref, vals, dv):
        dve = dv + BF16_SUBLANES
        for hd in range(vals.shape[0] // dv):
            v_ref[0, 0, hd * dve:hd * dve + dv, :] = vals[hd * dv:(hd + 1) * dv].astype(BF16)
            v_ref[0, 0, hd * dve + dv:(hd + 1) * dve, :] = ones_rows

    off = 0
    r = section(off, a_width)
    for hd in range(a_width // HEAD_DIM):
        blk = r[hd * HEAD_DIM:(hd + 1) * HEAD_DIM]
        y = _rope_axial(head_norm(blk, qn_ref[...]), ca, sa) * qscale
        qa_ref[0, 0, hd * HEAD_DIM:(hd + 1) * HEAD_DIM, :] = y.astype(BF16)
    off += a_width

    r = section(off, 2 * a_kv)
    ks = []
    for hd in range(a_kv // HEAD_DIM):
        blk = r[hd * HEAD_DIM:(hd + 1) * HEAD_DIM]
        ks.append(_rope_axial(head_norm(blk, kn_ref[...]), ca, sa))
    ka_ref[0] = jnp.concatenate(ks, axis=0).T.astype(BF16)
    store_values(va_ref, r[a_kv:2 * a_kv], HEAD_DIM)
    off += 2 * a_kv

    ga_ref[0, 0] = _silu(section(off, a_width)).astype(BF16)
    off += a_width

    r = section(off, b_qk)
    for hd in range(b_qk // HEAD_DIM):
        blk = r[hd * HEAD_DIM:(hd + 1) * HEAD_DIM]
        qb_ref[0, 0, hd * HEAD_DIM:(hd + 1) * HEAD_DIM, :] = (_rope_1d(blk, cb, sb) * qscale).astype(BF16)
    off += b_qk

    r = section(off, b_qk)
    ks = [_rope_1d(r[hd * HEAD_DIM:(hd + 1) * HEAD_DIM], cb, sb) for hd in range(b_qk // HEAD_DIM)]
    kb_ref[0] = jnp.concatenate(ks, axis=0).T.astype(BF16)
    off += b_qk

    store_values(vb_ref, section(off, b_width), 2 * HEAD_DIM)
    off += b_width
    gb_ref[0, 0] = _silu(section(off, b_width)).astype(BF16)


def _attn_kernel(*refs, mode, n_chunks, lam_init, unroll):
    if mode == "A":
        q_ref, k_ref, v_ref, g_ref, o_ref, s0_ref, s1_ref, acc_ref = refs
        dv = HEAD_DIM
    else:
        q_ref, k_ref, v_ref, g_ref, lam_ref, sub_ref, o_ref, s0_ref, s1_ref, acc_ref = refs
        dv = 2 * HEAD_DIM
    dve = dv + BF16_SUBLANES
    s_refs = (s0_ref, s1_ref)
    nq, tk = q_ref.shape[1], v_ref.shape[-1]
    tq = nq * tk
    slots = k_ref.shape[-1] // HEAD_DIM

    def tiles(ref):
        return jnp.concatenate([ref[0, j] for j in range(nq)], axis=1)

    def store_tiles(rows, val):
        for j in range(nq):
            o_ref[0, j, rows, :] = val[:, j * tk:(j + 1) * tk].astype(BF16)

    q = tiles(q_ref)
    z = jnp.zeros((HEAD_DIM, tq), BF16)

    def place(i, slot):
        rows = [z] * slots
        rows[slot] = q[i * HEAD_DIM:(i + 1) * HEAD_DIM]
        return jnp.concatenate(rows, axis=0)

    if mode == "A":
        first = pl.program_id(1) == 0
        qx = [jnp.where(first, place(i, 0), place(i, 1)) for i in range(MAPS_PER_STEP)]
        v_rows = [slice(0, dve)] * MAPS_PER_STEP
    else:
        qx = [place(i, i) for i in range(MAPS_PER_STEP)]
        v_rows = [slice((i // 2) * dve, (i // 2 + 1) * dve) for i in range(MAPS_PER_STEP)]

    n_sub = tk // SUB_KEYS

    def keys(c, j):
        return pl.ds(pl.multiple_of(c * tk + j * SUB_KEYS, SUB_KEYS), SUB_KEYS)

    def pass1(i, c, j, m8):
        s = _dot(k_ref[0, keys(c, j), :], qx[i])
        s_refs[i % 2][keys(c, j), :] = s
        return jnp.maximum(m8, jnp.max(s.reshape(SUB_KEYS // 8, 8, tq), axis=0))

    def pass2(i, c, j, mb):
        s = s_refs[i % 2][keys(c, j), :]
        p = jnp.exp2(s.reshape(SUB_KEYS // 8, 8, tq) - mb[None]).reshape(SUB_KEYS, tq).astype(BF16)
        vs = v_ref[0, c, v_rows[i], j * SUB_KEYS:(j + 1) * SUB_KEYS]
        return _dot(vs, p)

    def col_max(m8):
        return jnp.broadcast_to(jnp.max(m8, axis=0, keepdims=True), (8, tq))

    def chunk(c, m8, i1, i2, mb):
        part = None
        for j in range(n_sub):
            if i1 is not None:
                m8 = pass1(i1, c, j, m8)
            if i2 is not None:
                d = pass2(i2, c, j, mb)
                part = d if part is None else part + d
        if i2 is not None:
            acc_ref[i2] += part
        return m8

    neg = jnp.full((8, tq), -jnp.inf, F32)
    acc_ref[...] = jnp.zeros(acc_ref.shape, F32)
    mb = None
    for i in range(MAPS_PER_STEP + 1):
        i1 = i if i < MAPS_PER_STEP else None
        i2 = i - 1 if i > 0 else None
        m8 = lax.fori_loop(0, n_chunks, functools.partial(chunk, i1=i1, i2=i2, mb=mb), neg, unroll=unroll)
        mb = col_max(m8)


    def normalised(i):
        acc = acc_ref[i]
        return acc[0:dv] * (1.0 / acc[dv:dv + 1])

    g = tiles(g_ref).astype(F32)
    if mode == "A":
        for i in range(MAPS_PER_STEP):
            rows = slice(i * dv, (i + 1) * dv)
            store_tiles(rows, normalised(i) * g[rows])
    else:
        lv = lam_ref[...]
        lam = (jnp.exp(jnp.sum(lv[0:1] * lv[1:2], axis=-1, keepdims=True))
               - jnp.exp(jnp.sum(lv[2:3] * lv[3:4], axis=-1, keepdims=True)) + lam_init)
        for j in range(MAPS_PER_STEP // 2):
            o = normalised(2 * j) - lam * normalised(2 * j + 1)
            y = o * lax.rsqrt(jnp.mean(o * o, axis=0, keepdims=True) + EPS) * sub_ref[...]
            y = y * (1.0 - lam_init)
            rows = slice(j * dv, (j + 1) * dv)
            store_tiles(rows, y * g[rows])


def _out_kernel(x_ref, oa_ref, ob_ref, w_ref, gf_ref, y_ref, *, a_width, final):
    y = x_ref[0] + _dot_tn(oa_ref[0, 0], w_ref[0:a_width, :]) + _dot_tn(ob_ref[0, 0], w_ref[a_width:, :])
    if final:
        y = y * lax.rsqrt(jnp.mean(y * y, axis=-1, keepdims=True) + EPS) * gf_ref[...]
    y_ref[0] = y


def _rope_tables(seq):
    def cs(pos, dim):
        inv = ROPE_THETA ** (-jnp.arange(0, dim, 2, dtype=F32) / dim)
        ang = pos[None, :] * inv[:, None]
        return jnp.cos(ang), jnp.sin(ang)

    rows = seq // GRID_W
    row = jnp.repeat(jnp.arange(rows, dtype=F32), GRID_W)
    col = jnp.tile(jnp.arange(GRID_W, dtype=F32), rows)
    pos = jnp.arange(seq, dtype=F32)
    cr, sr = cs(row, HEAD_DIM // 2)
    cc, sc = cs(col, HEAD_DIM // 2)
    cp, sp = cs(pos, HEAD_DIM)
    ca = jnp.concatenate([cr, cr, cc, cc], axis=0)
    sa = jnp.concatenate([-sr, sr, -sc, sc], axis=0)
    cb = jnp.concatenate([cp, cp], axis=0)
    sb = jnp.concatenate([-sp, sp], axis=0)
    return ca, sa, cb, sb


def _layer(x, gn, wt, qn, kn, lam_vecs, subln, w_out, g_final, lam_init, final):
    bsz, seq, d = x.shape
    a_width = (d // (2 * HEAD_DIM)) * HEAD_DIM
    a_kv = max(1, (a_width // HEAD_DIM) // 4) * HEAD_DIM
    b_qk = (d // (4 * HEAD_DIM)) * 2 * HEAD_DIM
    b_width = b_qk
    assert a_width == MAPS_PER_STEP * a_kv and a_width % STEP_ROWS == 0 and b_width % STEP_ROWS == 0
    t = TOKEN_TILE
    assert seq % t == 0 and seq % GRID_W == 0
    nc = seq // t
    va_rows = (a_kv // HEAD_DIM) * (HEAD_DIM + BF16_SUBLANES)
    vb_rows = (b_width // (2 * HEAD_DIM)) * (2 * HEAD_DIM + BF16_SUBLANES)
    ca, sa, cb, sb = _rope_tables(seq)
    cparams = functools.partial(pltpu.CompilerParams, vmem_limit_bytes=VMEM_LIMIT_BYTES)

    def fm(rows):
        return jax.ShapeDtypeStruct((bsz, nc, rows, t), BF16)

    fm_spec = lambda rows: pl.BlockSpec((1, 1, rows, t), lambda b, i: (b, i, 0, 0))
    tab_spec = pl.BlockSpec((HEAD_DIM, t), lambda b, i: (0, i))
    full = lambda shape: pl.BlockSpec(shape, lambda b, i: (0,) * len(shape))

    qa, va, ga, qb, vb, gb, ka, kb = pl.pallas_call(
        functools.partial(_proj_kernel, sizes=(a_width, a_kv, b_qk, b_width)),
        grid=(bsz, nc),
        in_specs=[pl.BlockSpec((1, t, d), lambda b, i: (b, i, 0)), full((1, d)), full(wt.shape),
                  full((HEAD_DIM, 1)), full((HEAD_DIM, 1)), tab_spec, tab_spec, tab_spec, tab_spec],
        out_specs=[fm_spec(a_width), fm_spec(va_rows), fm_spec(a_width), fm_spec(b_qk), fm_spec(vb_rows),
                   fm_spec(b_width),
                   pl.BlockSpec((1, t, a_kv), lambda b, i: (b, i, 0)),
                   pl.BlockSpec((1, t, b_qk), lambda b, i: (b, i, 0))],
        out_shape=[fm(a_width), fm(va_rows), fm(a_width), fm(b_qk), fm(vb_rows), fm(b_width),
                   jax.ShapeDtypeStruct((bsz, seq, a_kv), BF16),
                   jax.ShapeDtypeStruct((bsz, seq, b_qk), BF16)],
        compiler_params=cparams(dimension_semantics=("parallel", "parallel")),
        name="proj",
    )(x, gn, wt, qn, kn, ca, sa, cb, sb)

    nq = max(1, min(nc, SCORE_MAP_ELEMS // (seq * t)))
    assert nc % nq == 0 and nc % 2 == 0
    tq = nq * t
    unroll = nc // 2
    step_spec = pl.BlockSpec((1, nq, STEP_ROWS, t), lambda b, u, i: (b, i, u, 0))
    sem = ("parallel", "parallel", "arbitrary")

    def scratch(dve):
        return [pltpu.VMEM((seq, tq), F32), pltpu.VMEM((seq, tq), F32), pltpu.VMEM((MAPS_PER_STEP, dve, tq), F32)]

    dve_a = HEAD_DIM + BF16_SUBLANES
    oa = pl.pallas_call(
        functools.partial(_attn_kernel, mode="A", n_chunks=nc, lam_init=lam_init, unroll=unroll),
        grid=(bsz, a_width // STEP_ROWS, nc // nq),
        in_specs=[step_spec,
                  pl.BlockSpec((1, seq, a_kv), lambda b, u, i: (b, 0, 0)),
                  pl.BlockSpec((1, nc, dve_a, t), lambda b, u, i: (b, 0, u, 0)),
                  step_spec],
        out_specs=step_spec,
        out_shape=fm(a_width),
        scratch_shapes=scratch(dve_a),
        compiler_params=cparams(dimension_semantics=sem),
        name="attn_a",
    )(qa, ka, va, ga)

    dve_b = 2 * HEAD_DIM + BF16_SUBLANES
    ob = pl.pallas_call(
        functools.partial(_attn_kernel, mode="B", n_chunks=nc, lam_init=lam_init, unroll=unroll),
        grid=(bsz, b_width // STEP_ROWS, nc // nq),
        in_specs=[step_spec,
                  pl.BlockSpec((1, seq, STEP_ROWS), lambda b, u, i: (b, 0, u)),
                  pl.BlockSpec((1, nc, 2 * dve_b, t), lambda b, u, i: (b, 0, u, 0)),
                  step_spec,
                  pl.BlockSpec((4, HEAD_DIM), lambda b, u, i: (0, 0)),
                  pl.BlockSpec((2 * HEAD_DIM, 1), lambda b, u, i: (0, 0))],
        out_specs=step_spec,
        out_shape=fm(b_width),
        scratch_shapes=scratch(dve_b),
        compiler_params=cparams(dimension_semantics=sem),
        name="attn_b",
    )(qb, kb, vb, gb, lam_vecs, subln)

    return pl.pallas_call(
        functools.partial(_out_kernel, a_width=a_width, final=final),
        grid=(bsz, nc),
        in_specs=[pl.BlockSpec((1, t, d), lambda b, i: (b, i, 0)), fm_spec(a_width), fm_spec(b_width),
                  full(w_out.shape), full((1, d))],
        out_specs=pl.BlockSpec((1, t, d), lambda b, i: (b, i, 0)),
        out_shape=jax.ShapeDtypeStruct(x.shape, x.dtype),
        compiler_params=cparams(dimension_semantics=("parallel", "parallel")),
        name="out_proj",
    )(x, oa, ob, w_out, g_final)


def kernel(x_prompt, x_sample, g_norm, w_in, a_q_norm, a_k_norm, b_lambda_q1, b_lambda_k1,
           b_lambda_q2, b_lambda_k2, b_subln, w_out, g_final):
    depth = w_in.shape[0]
    hp, hs = x_prompt, x_sample
    for l in range(depth):
        lam_init = 0.8 - 0.6 * math.exp(-0.3 * l)
        args = (g_norm[l][None, :], w_in[l].T.astype(BF16), a_q_norm[l][:, None], a_k_norm[l][:, None],
                jnp.stack([b_lambda_q1[l], b_lambda_k1[l], b_lambda_q2[l], b_lambda_k2[l]]).astype(F32),
                b_subln[l][:, None], w_out[l].astype(BF16), g_final[None, :])
        final = l == depth - 1
        hp = _layer(hp, *args, lam_init, final)
        hs = _layer(hs, *args, lam_init, final)
    return (hp, hs)
```

```python
import functools
import math

import jax
import jax.numpy as jnp
from jax import lax
from jax.experimental import pallas as pl
from jax.experimental.pallas import tpu as pltpu

HEAD_DIM = 64
GRID_W = 64
ROPE_THETA = 10000.0
EPS = 1e-6
TOKEN_TILE = 512
SUB_KEYS = 256
MAPS_PER_STEP = 4
STEP_ROWS = MAPS_PER_STEP * HEAD_DIM
BF16_SUBLANES = 16
VMEM_LIMIT_BYTES = 56 * 1024 * 1024
LOG2E = math.log2(math.e)

F32 = jnp.float32
BF16 = jnp.bfloat16


def _dot_nt(a, b):
    return lax.dot_general(a, b, (((1,), (1,)), ((), ())), preferred_element_type=F32)


def _dot_tn(a, b):
    return lax.dot_general(a, b, (((0,), (0,)), ((), ())), preferred_element_type=F32)


def _dot(a, b):
    return jnp.dot(a, b, preferred_element_type=F32)


def _rope_axial(x, c, s):
    xs = jnp.concatenate([x[16:32], x[0:16], x[48:64], x[32:48]], axis=0)
    return x * c + xs * s


def _rope_1d(x, c, s):
    xs = jnp.concatenate([x[32:64], x[0:32]], axis=0)
    return x * c + xs * s


def _silu(g):
    return g * (1.0 / (1.0 + jnp.exp(-g)))


def _proj_kernel(x_ref, gn_ref, wt_ref, qn_ref, kn_ref, ca_ref, sa_ref, cb_ref, sb_ref,
                 qa_ref, va_ref, ga_ref, qb_ref, vb_ref, gb_ref, ka_ref, kb_ref, *, sizes):
    a_width, a_kv, b_qk, b_width = sizes
    x = x_ref[0]
    t = x.shape[0]
    h = x * lax.rsqrt(jnp.mean(x * x, axis=-1, keepdims=True) + EPS) * gn_ref[...]
    hb = h.astype(BF16)

    def section(lo, n):
        return _dot_nt(wt_ref[lo:lo + n, :], hb)

    ca, sa, cb, sb = ca_ref[...], sa_ref[...], cb_ref[...], sb_ref[...]
    qscale = HEAD_DIM ** -0.5 * LOG2E
    ones_rows = (lax.broadcasted_iota(jnp.int32, (BF16_SUBLANES, t), 0) == 0).astype(BF16)

    def head_norm(blk, g):
        return blk * lax.rsqrt(jnp.mean(blk * blk, axis=0, keepdims=True) + EPS) * g

    def store_values(v_ref, vals, dv):
        dve = dv + BF16_SUBLANES
        for hd in range(vals.shape[0] // dv):
            v_ref[0, 0, hd * dve:hd * dve + dv, :] = vals[hd * dv:(hd + 1) * dv].astype(BF16)
            v_ref[0, 0, hd * dve + dv:(hd + 1) * dve, :] = ones_rows

    off = 0
    r = section(off, a_width)
    for hd in range(a_width // HEAD_DIM):
        blk = r[hd * HEAD_DIM:(hd + 1) * HEAD_DIM]
        y = _rope_axial(head_norm(blk, qn_ref[...]), ca, sa) * qscale
        qa_ref[0, 0, hd * HEAD_DIM:(hd + 1) * HEAD_DIM, :] = y.astype(BF16)
    off += a_width

    r = section(off, 2 * a_kv)
    ks = []
    for hd in range(a_kv // HEAD_DIM):
        blk = r[hd * HEAD_DIM:(hd + 1) * HEAD_DIM]
        ks.append(_rope_axial(head_norm(blk, kn_ref[...]), ca, sa))
    ka_ref[0] = jnp.concatenate(ks, axis=0).T.astype(BF16)
    store_values(va_ref, r[a_kv:2 * a_kv], HEAD_DIM)
    off += 2 * a_kv

    ga_ref[0, 0] = _silu(section(off, a_width)).astype(BF16)
    off += a_width

    r = section(off, b_qk)
    for hd in range(b_qk // HEAD_DIM):
        blk = r[hd * HEAD_DIM:(hd + 1) * HEAD_DIM]
        qb_ref[0, 0, hd * HEAD_DIM:(hd + 1) * HEAD_DIM, :] = (_rope_1d(blk, cb, sb) * qscale).astype(BF16)
    off += b_qk

    r = section(off, b_qk)
    ks = [_rope_1d(r[hd * HEAD_DIM:(hd + 1) * HEAD_DIM], cb, sb) for hd in range(b_qk // HEAD_DIM)]
    kb_ref[0] = jnp.concatenate(ks, axis=0).T.astype(BF16)
    off += b_qk

    store_values(vb_ref, section(off, b_width), 2 * HEAD_DIM)
    off += b_width
    gb_ref[0, 0] = _silu(section(off, b_width)).astype(BF16)


def _attn_kernel(*refs, mode, n_chunks, lam_init, grid_dims):
    if mode == "A":
        q_ref, qn_ref, k_ref, kn_ref, v_ref, g_ref, o_ref, s0_ref, s1_ref, acc_ref, mcar_ref = refs
        dv = HEAD_DIM
    else:
        (q_ref, qn_ref, k_ref, kn_ref, v_ref, g_ref, lam_ref, sub_ref, o_ref,
         s0_ref, s1_ref, acc_ref, mcar_ref) = refs
        dv = 2 * HEAD_DIM
    dve = dv + BF16_SUBLANES
    s_refs = (s0_ref, s1_ref)
    tq, tk = q_ref.shape[-1], v_ref.shape[-1]
    slots = k_ref.shape[-1] // HEAD_DIM
    n_sub = tk // SUB_KEYS
    _, nu, ni = grid_dims
    b, u, i = pl.program_id(0), pl.program_id(1), pl.program_id(2)
    is_first = (b == 0) & (u == 0) & (i == 0)
    u_next = (u + (i + 1) // ni) % nu
    z = jnp.zeros((HEAD_DIM, tq), BF16)

    def place(rows64, slot):
        rows = [z] * slots
        rows[slot] = rows64
        return jnp.concatenate(rows, axis=0)

    def expand(rows64, map_idx, unit):
        if mode == "A":
            return jnp.where(unit == 0, place(rows64, 0), place(rows64, 1))
        return place(rows64, map_idx)

    q = q_ref[0, 0]
    qx = [expand(q[m * HEAD_DIM:(m + 1) * HEAD_DIM], m, u) for m in range(MAPS_PER_STEP)]
    qx_next = expand(qn_ref[0, 0], 0, u_next)
    v_rows = [slice(0, dve) if mode == "A" else slice((m // 2) * dve, (m // 2 + 1) * dve)
              for m in range(MAPS_PER_STEP)]

    def keys(c, j):
        return pl.ds(c * tk + j * SUB_KEYS, SUB_KEYS)

    def pass1(kref, qxm, s_dst, c, j, m8):
        s = _dot(kref[0, keys(c, j), :], qxm)
        s_dst[keys(c, j), :] = s
        return jnp.maximum(m8, jnp.max(s.reshape(SUB_KEYS // 8, 8, tq), axis=0))

    def pass2(m, s_src, mb, c, j):
        s = s_src[keys(c, j), :]
        p = jnp.exp2(s.reshape(SUB_KEYS // 8, 8, tq) - mb[None]).reshape(SUB_KEYS, tq).astype(BF16)
        vs = v_ref[0, c, v_rows[m], j * SUB_KEYS:(j + 1) * SUB_KEYS]
        return _dot(vs, p)

    neg = jnp.full((8, tq), -jnp.inf, F32)

    def run_phase(trips, p1, p2):
        def phase(_, m8):
            for c in range(n_chunks):
                part = None
                for j in range(n_sub):
                    if p1 is not None:
                        m8 = pass1(*p1, c, j, m8)
                    if p2 is not None:
                        d = pass2(*p2, c, j)
                        part = d if part is None else part + d
                if p2 is not None:
                    acc_ref[p2[0]] += part
            return m8

        return lax.fori_loop(0, trips, phase, neg)

    def col_max(m8):
        return jnp.broadcast_to(jnp.max(m8, axis=0, keepdims=True), (8, tq))

    acc_ref[...] = jnp.zeros(acc_ref.shape, F32)
    m_first = run_phase(is_first.astype(jnp.int32), (k_ref, qx[0], s_refs[0]), None)

    @pl.when(is_first)
    def _():
        mcar_ref[...] = m_first

    m8 = mcar_ref[...]
    once = jnp.minimum(i, 0) + 1
    for m in range(1, MAPS_PER_STEP):
        m8 = run_phase(once, (k_ref, qx[m], s_refs[m % 2]), (m - 1, s_refs[(m - 1) % 2], col_max(m8)))
    last = MAPS_PER_STEP - 1
    mcar_ref[...] = run_phase(once, (kn_ref, qx_next, s_refs[0]), (last, s_refs[last % 2], col_max(m8)))

    def normalised(m):
        acc = acc_ref[m]
        return acc[0:dv] * (1.0 / acc[dv:dv + 1])

    g = g_ref[0, 0].astype(F32)
    if mode == "A":
        for m in range(MAPS_PER_STEP):
            rows = slice(m * dv, (m + 1) * dv)
            o_ref[0, 0, rows, :] = (normalised(m) * g[rows]).astype(BF16)
    else:
        lv = lam_ref[...]
        lam = (jnp.exp(jnp.sum(lv[0:1] * lv[1:2], axis=-1, keepdims=True))
               - jnp.exp(jnp.sum(lv[2:3] * lv[3:4], axis=-1, keepdims=True)) + lam_init)
        for j in range(MAPS_PER_STEP // 2):
            o = normalised(2 * j) - lam * normalised(2 * j + 1)
            y = o * lax.rsqrt(jnp.mean(o * o, axis=0, keepdims=True) + EPS) * sub_ref[...]
            y = y * (1.0 - lam_init)
            rows = slice(j * dv, (j + 1) * dv)
            o_ref[0, 0, rows, :] = (y * g[rows]).astype(BF16)


def _out_kernel(x_ref, oa_ref, ob_ref, w_ref, gf_ref, y_ref, *, a_width, final):
    y = x_ref[0] + _dot_tn(oa_ref[0, 0], w_ref[0:a_width, :]) + _dot_tn(ob_ref[0, 0], w_ref[a_width:, :])
    if final:
        y = y * lax.rsqrt(jnp.mean(y * y, axis=-1, keepdims=True) + EPS) * gf_ref[...]
    y_ref[0] = y


def _rope_tables(seq):
    def cs(pos, dim):
        inv = ROPE_THETA ** (-jnp.arange(0, dim, 2, dtype=F32) / dim)
        ang = pos[None, :] * inv[:, None]
        return jnp.cos(ang), jnp.sin(ang)

    rows = seq // GRID_W
    row = jnp.repeat(jnp.arange(rows, dtype=F32), GRID_W)
    col = jnp.tile(jnp.arange(GRID_W, dtype=F32), rows)
    pos = jnp.arange(seq, dtype=F32)
    cr, sr = cs(row, HEAD_DIM // 2)
    cc, sc = cs(col, HEAD_DIM // 2)
    cp, sp = cs(pos, HEAD_DIM)
    ca = jnp.concatenate([cr, cr, cc, cc], axis=0)
    sa = jnp.concatenate([-sr, sr, -sc, sc], axis=0)
    cb = jnp.concatenate([cp, cp], axis=0)
    sb = jnp.concatenate([-sp, sp], axis=0)
    return ca, sa, cb, sb


def _layer(x, gn, wt, qn, kn, lam_vecs, subln, w_out, g_final, lam_init, final):
    bsz, seq, d = x.shape
    a_width = (d // (2 * HEAD_DIM)) * HEAD_DIM
    a_kv = max(1, (a_width // HEAD_DIM) // 4) * HEAD_DIM
    b_qk = (d // (4 * HEAD_DIM)) * 2 * HEAD_DIM
    b_width = b_qk
    assert a_width == MAPS_PER_STEP * a_kv and a_width % STEP_ROWS == 0 and b_width % STEP_ROWS == 0
    t = TOKEN_TILE
    assert seq % t == 0 and seq % GRID_W == 0
    nc = seq // t
    va_rows = (a_kv // HEAD_DIM) * (HEAD_DIM + BF16_SUBLANES)
    vb_rows = (b_width // (2 * HEAD_DIM)) * (2 * HEAD_DIM + BF16_SUBLANES)
    ca, sa, cb, sb = _rope_tables(seq)
    cparams = functools.partial(pltpu.CompilerParams, vmem_limit_bytes=VMEM_LIMIT_BYTES)

    def fm(rows):
        return jax.ShapeDtypeStruct((bsz, nc, rows, t), BF16)

    fm_spec = lambda rows: pl.BlockSpec((1, 1, rows, t), lambda b, i: (b, i, 0, 0))
    tab_spec = pl.BlockSpec((HEAD_DIM, t), lambda b, i: (0, i))
    full = lambda shape: pl.BlockSpec(shape, lambda b, i: (0,) * len(shape))

    qa, va, ga, qb, vb, gb, ka, kb = pl.pallas_call(
        functools.partial(_proj_kernel, sizes=(a_width, a_kv, b_qk, b_width)),
        grid=(bsz, nc),
        in_specs=[pl.BlockSpec((1, t, d), lambda b, i: (b, i, 0)), full((1, d)), full(wt.shape),
                  full((HEAD_DIM, 1)), full((HEAD_DIM, 1)), tab_spec, tab_spec, tab_spec, tab_spec],
        out_specs=[fm_spec(a_width), fm_spec(va_rows), fm_spec(a_width), fm_spec(b_qk), fm_spec(vb_rows),
                   fm_spec(b_width),
                   pl.BlockSpec((1, t, a_kv), lambda b, i: (b, i, 0)),
                   pl.BlockSpec((1, t, b_qk), lambda b, i: (b, i, 0))],
        out_shape=[fm(a_width), fm(va_rows), fm(a_width), fm(b_qk), fm(vb_rows), fm(b_width),
                   jax.ShapeDtypeStruct((bsz, seq, a_kv), BF16),
                   jax.ShapeDtypeStruct((bsz, seq, b_qk), BF16)],
        compiler_params=cparams(dimension_semantics=("parallel", "parallel")),
        name="proj",
    )(x, gn, wt, qn, kn, ca, sa, cb, sb)

    grid_a = (bsz, a_width // STEP_ROWS, nc)
    grid_b = (bsz, b_width // STEP_ROWS, nc)

    def next_step(dims):
        nb, nu, ni = dims

        def f(b, u, i):
            cu = (i + 1) // ni
            cb = (u + cu) // nu
            return jnp.minimum(b + cb, nb - 1), (u + cu) % nu, (i + 1) % ni
        return f

    step_spec = pl.BlockSpec((1, 1, STEP_ROWS, t), lambda b, u, i: (b, i, u, 0))
    sem = ("arbitrary", "arbitrary", "arbitrary")

    def next_q_spec(dims):
        nxt = next_step(dims)

        def index(b, u, i):
            b2, u2, i2 = nxt(b, u, i)
            return b2, i2, u2 * MAPS_PER_STEP, 0
        return pl.BlockSpec((1, 1, HEAD_DIM, t), index)

    def scratch(dve):
        return [pltpu.VMEM((seq, t), F32), pltpu.VMEM((seq, t), F32),
                pltpu.VMEM((MAPS_PER_STEP, dve, t), F32), pltpu.VMEM((8, t), F32)]

    dve_a = HEAD_DIM + BF16_SUBLANES
    ka_spec = lambda index: pl.BlockSpec((1, seq, a_kv), index)
    oa = pl.pallas_call(
        functools.partial(_attn_kernel, mode="A", n_chunks=nc, lam_init=lam_init, grid_dims=grid_a),
        grid=grid_a,
        in_specs=[step_spec, next_q_spec(grid_a),
                  ka_spec(lambda b, u, i: (b, 0, 0)),
                  ka_spec(lambda b, u, i: (next_step(grid_a)(b, u, i)[0], 0, 0)),
                  pl.BlockSpec((1, nc, dve_a, t), lambda b, u, i: (b, 0, u, 0)),
                  step_spec],
        out_specs=step_spec,
        out_shape=fm(a_width),
        scratch_shapes=scratch(dve_a),
        compiler_params=cparams(dimension_semantics=sem),
        name="attn_a",
    )(qa, qa, ka, ka, va, ga)

    dve_b = 2 * HEAD_DIM + BF16_SUBLANES
    kb_spec = lambda index: pl.BlockSpec((1, seq, STEP_ROWS), index)

    def kb_next(b, u, i):
        b2, u2, _ = next_step(grid_b)(b, u, i)
        return b2, 0, u2

    ob = pl.pallas_call(
        functools.partial(_attn_kernel, mode="B", n_chunks=nc, lam_init=lam_init, grid_dims=grid_b),
        grid=grid_b,
        in_specs=[step_spec, next_q_spec(grid_b),
                  kb_spec(lambda b, u, i: (b, 0, u)),
                  kb_spec(kb_next),
                  pl.BlockSpec((1, nc, 2 * dve_b, t), lambda b, u, i: (b, 0, u, 0)),
                  step_spec,
                  pl.BlockSpec((4, HEAD_DIM), lambda b, u, i: (0, 0)),
                  pl.BlockSpec((2 * HEAD_DIM, 1), lambda b, u, i: (0, 0))],
        out_specs=step_spec,
        out_shape=fm(b_width),
        scratch_shapes=scratch(dve_b),
        compiler_params=cparams(dimension_semantics=sem),
        name="attn_b",
    )(qb, qb, kb, kb, vb, gb, lam_vecs, subln)

    return pl.pallas_call(
        functools.partial(_out_kernel, a_width=a_width, final=final),
        grid=(bsz, nc),
        in_specs=[pl.BlockSpec((1, t, d), lambda b, i: (b, i, 0)), fm_spec(a_width), fm_spec(b_width),
                  full(w_out.shape), full((1, d))],
        out_specs=pl.BlockSpec((1, t, d), lambda b, i: (b, i, 0)),
        out_shape=jax.ShapeDtypeStruct(x.shape, x.dtype),
        compiler_params=cparams(dimension_semantics=("parallel", "parallel")),
        name="out_proj",
    )(x, oa, ob, w_out, g_final)


def kernel(x_prompt, x_sample, g_norm, w_in, a_q_norm, a_k_norm, b_lambda_q1, b_lambda_k1,
           b_lambda_q2, b_lambda_k2, b_subln, w_out, g_final):
    depth = w_in.shape[0]
    hp, hs = x_prompt, x_sample
    for l in range(depth):
        lam_init = 0.8 - 0.6 * math.exp(-0.3 * l)
        args = (g_norm[l][None, :], w_in[l].T.astype(BF16), a_q_norm[l][:, None], a_k_norm[l][:, None],
                jnp.stack([b_lambda_q1[l], b_lambda_k1[l], b_lambda_q2[l], b_lambda_k2[l]]).astype(F32),
                b_subln[l][:, None], w_out[l].astype(BF16), g_final[None, :])
        final = l == depth - 1
        hp = _layer(hp, *args, lam_init, final)
        hs = _layer(hs, *args, lam_init, final)
    return (hp, hs)
```

```python
import functools
import math

import jax
import jax.numpy as jnp
from jax import lax
from jax.experimental import pallas as pl
from jax.experimental.pallas import tpu as pltpu

HEAD_DIM = 64
GRID_W = 64
ROPE_THETA = 10000.0
EPS = 1e-6
TOKEN_TILE = 512
SUB_KEYS = 256
HEADS_PER_STEP = 4
STEP_ROWS = HEADS_PER_STEP * HEAD_DIM
PHASE_PAIRS = 16
BF16_SUBLANES = 16
VMEM_LIMIT_BYTES = 56 * 1024 * 1024
LOG2E = math.log2(math.e)

F32 = jnp.float32
BF16 = jnp.bfloat16


def _dot_nt(a, b):
    return lax.dot_general(a, b, (((1,), (1,)), ((), ())), preferred_element_type=F32)


def _dot_tn(a, b):
    return lax.dot_general(a, b, (((0,), (0,)), ((), ())), preferred_element_type=F32)


def _dot(a, b):
    return jnp.dot(a, b, preferred_element_type=F32)


def _rope_axial(x, c, s):
    xs = jnp.concatenate([x[16:32], x[0:16], x[48:64], x[32:48]], axis=0)
    return x * c + xs * s


def _rope_1d(x, c, s):
    xs = jnp.concatenate([x[32:64], x[0:32]], axis=0)
    return x * c + xs * s


def _silu(g):
    return g * (1.0 / (1.0 + jnp.exp(-g)))


def _proj_kernel(x_ref, gn_ref, wt_ref, qn_ref, kn_ref, ca_ref, sa_ref, cb_ref, sb_ref,
                 qa_ref, va_ref, ga_ref, qb_ref, vb_ref, gb_ref, ka_ref, kb_ref, *, sizes):
    a_width, a_kv, b_qk, b_width = sizes
    x = x_ref[0]
    t = x.shape[0]
    h = x * lax.rsqrt(jnp.mean(x * x, axis=-1, keepdims=True) + EPS) * gn_ref[...]
    hb = h.astype(BF16)

    def section(lo, n):
        return _dot_nt(wt_ref[lo:lo + n, :], hb)

    ca, sa, cb, sb = ca_ref[...], sa_ref[...], cb_ref[...], sb_ref[...]
    qscale = HEAD_DIM ** -0.5 * LOG2E
    ones_rows = (lax.broadcasted_iota(jnp.int32, (BF16_SUBLANES, t), 0) == 0).astype(BF16)

    def head_norm(blk, g):
        return blk * lax.rsqrt(jnp.mean(blk * blk, axis=0, keepdims=True) + EPS) * g

    def store_values(v_ref, vals, dv):
        dve = dv + BF16_SUBLANES
        for hd in range(vals.shape[0] // dv):
            v_ref[0, 0, hd * dve:hd * dve + dv, :] = vals[hd * dv:(hd + 1) * dv].astype(BF16)
            v_ref[0, 0, hd * dve + dv:(hd + 1) * dve, :] = ones_rows

    off = 0
    r = section(off, a_width)
    for hd in range(a_width // HEAD_DIM):
        blk = r[hd * HEAD_DIM:(hd + 1) * HEAD_DIM]
        y = _rope_axial(head_norm(blk, qn_ref[...]), ca, sa) * qscale
        qa_ref[0, 0, hd * HEAD_DIM:(hd + 1) * HEAD_DIM, :] = y.astype(BF16)
    off += a_width

    r = section(off, 2 * a_kv)
    ks = []
    for hd in range(a_kv // HEAD_DIM):
        blk = r[hd * HEAD_DIM:(hd + 1) * HEAD_DIM]
        ks.append(_rope_axial(head_norm(blk, kn_ref[...]), ca, sa))
    ka_ref[0] = jnp.concatenate(ks, axis=0).T.astype(BF16)
    store_values(va_ref, r[a_kv:2 * a_kv], HEAD_DIM)
    off += 2 * a_kv

    ga_ref[0, 0] = _silu(section(off, a_width)).astype(BF16)
    off += a_width

    r = section(off, b_qk)
    for hd in range(b_qk // HEAD_DIM):
        blk = r[hd * HEAD_DIM:(hd + 1) * HEAD_DIM]
        qb_ref[0, 0, hd * HEAD_DIM:(hd + 1) * HEAD_DIM, :] = (_rope_1d(blk, cb, sb) * qscale).astype(BF16)
    off += b_qk

    r = section(off, b_qk)
    ks = [_rope_1d(r[hd * HEAD_DIM:(hd + 1) * HEAD_DIM], cb, sb) for hd in range(b_qk // HEAD_DIM)]
    kb_ref[0] = jnp.concatenate(ks, axis=0).T.astype(BF16)
    off += b_qk

    store_values(vb_ref, section(off, b_width), 2 * HEAD_DIM)
    off += b_width
    gb_ref[0, 0] = _silu(section(off, b_width)).astype(BF16)


def _attn_kernel(*refs, mode, n_chunks, lam_init, grid_dims, group):
    n_s = 2 * group
    if mode == "A":
        q_ref, qn_ref, k_ref, kn_ref, v_ref, g_ref, o_ref = refs[:7]
        dv = HEAD_DIM
    else:
        q_ref, qn_ref, k_ref, kn_ref, v_ref, g_ref, lam_ref, sub_ref, o_ref = refs[:9]
        dv = 2 * HEAD_DIM
    s_refs, (acc_ref, mcar_ref) = refs[-2 - n_s:-2], refs[-2:]
    dve = dv + BF16_SUBLANES
    nq, tq, tk = q_ref.shape[1], q_ref.shape[-1], v_ref.shape[-1]
    n_maps = nq * HEADS_PER_STEP
    assert n_maps == 2 * group
    slots = k_ref.shape[-1] // HEAD_DIM
    n_sub = tk // SUB_KEYS
    _, nu, ni = grid_dims
    b, u, i = pl.program_id(0), pl.program_id(1), pl.program_id(2)
    is_first = (b == 0) & (u == 0) & (i == 0)
    u_next = (u + (i + 1) // ni) % nu
    z = jnp.zeros((HEAD_DIM, tq), BF16)

    def place(rows64, slot):
        rows = [z] * slots
        rows[slot] = rows64
        return jnp.concatenate(rows, axis=0)

    def expand(rows64, head, unit):
        if mode == "A":
            return jnp.where(unit == 0, place(rows64, 0), place(rows64, 1))
        return place(rows64, head)

    def head_rows(m):
        hd = m % HEADS_PER_STEP
        return slice(hd * HEAD_DIM, (hd + 1) * HEAD_DIM)

    qx = [expand(q_ref[0, m // HEADS_PER_STEP, head_rows(m), :], m % HEADS_PER_STEP, u) for m in range(n_maps)]
    qx_next = [expand(qn_ref[0, 0, head_rows(m), :], m, u_next) for m in range(group)]
    v_rows = [slice(0, dve) if mode == "A" else slice(((m % HEADS_PER_STEP) // 2) * dve,
                                                      ((m % HEADS_PER_STEP) // 2 + 1) * dve)
              for m in range(n_maps)]

    def keys(c, j):
        return pl.ds(c * tk + j * SUB_KEYS, SUB_KEYS)

    def pass1(kref, qxm, s_dst, c, j, m8):
        s = _dot(kref[0, keys(c, j), :], qxm)
        s_dst[keys(c, j), :] = s
        return jnp.maximum(m8, jnp.max(s.reshape(SUB_KEYS // 8, 8, tq), axis=0))

    def pass2(m, s_src, mb, c, j):
        s = s_src[keys(c, j), :]
        p = jnp.exp2(s.reshape(SUB_KEYS // 8, 8, tq) - mb[None]).reshape(SUB_KEYS, tq).astype(BF16)
        vs = v_ref[0, c, v_rows[m], j * SUB_KEYS:(j + 1) * SUB_KEYS]
        return _dot(vs, p)

    neg = jnp.full((8, tq), -jnp.inf, F32)

    def run_phase(trips, p1s, p2s):
        def phase(_, m8s):
            m8s = list(m8s)
            for c in range(n_chunks):
                parts = [None] * len(p2s)
                for j in range(n_sub):
                    for k in range(max(len(p1s), len(p2s))):
                        if k < len(p1s):
                            m8s[k] = pass1(*p1s[k], c, j, m8s[k])
                        if k < len(p2s):
                            d = pass2(*p2s[k], c, j)
                            parts[k] = d if parts[k] is None else parts[k] + d
                for k, p2 in enumerate(p2s):
                    acc_ref[p2[0]] += parts[k]
            return tuple(m8s)

        return lax.fori_loop(0, trips, phase, (neg,) * len(p1s))

    def col_max(m8):
        return jnp.broadcast_to(jnp.max(m8, axis=0, keepdims=True), (8, tq))

    def s_buf(parity, k):
        return s_refs[parity * group + k]

    acc_ref[...] = jnp.zeros(acc_ref.shape, F32)
    m_first = run_phase(is_first.astype(jnp.int32), [(k_ref, qx[k], s_buf(0, k)) for k in range(group)], [])

    @pl.when(is_first)
    def _():
        for k in range(group):
            mcar_ref[k] = m_first[k]

    once = jnp.minimum(i, 0) + 1
    m8s = [mcar_ref[k] for k in range(group)]
    m8s = run_phase(once, [(k_ref, qx[group + k], s_buf(1, k)) for k in range(group)],
                    [(k, s_buf(0, k), col_max(m8s[k])) for k in range(group)])
    m8s = run_phase(once, [(kn_ref, qx_next[k], s_buf(0, k)) for k in range(group)],
                    [(group + k, s_buf(1, k), col_max(m8s[k])) for k in range(group)])
    for k in range(group):
        mcar_ref[k] = m8s[k]

    def normalised(m):
        acc = acc_ref[m]
        return acc[0:dv] * (1.0 / acc[dv:dv + 1])

    if mode == "B":
        lv = lam_ref[...]
        lam = (jnp.exp(jnp.sum(lv[0:1] * lv[1:2], axis=-1, keepdims=True))
               - jnp.exp(jnp.sum(lv[2:3] * lv[3:4], axis=-1, keepdims=True)) + lam_init)
    for t_idx in range(nq):
        g = g_ref[0, t_idx].astype(F32)
        base = t_idx * HEADS_PER_STEP
        if mode == "A":
            for hd in range(HEADS_PER_STEP):
                rows = slice(hd * dv, (hd + 1) * dv)
                o_ref[0, t_idx, rows, :] = (normalised(base + hd) * g[rows]).astype(BF16)
        else:
            for j in range(HEADS_PER_STEP // 2):
                o = normalised(base + 2 * j) - lam * normalised(base + 2 * j + 1)
                y = o * lax.rsqrt(jnp.mean(o * o, axis=0, keepdims=True) + EPS) * sub_ref[...]
                y = y * (1.0 - lam_init)
                rows = slice(j * dv, (j + 1) * dv)
                o_ref[0, t_idx, rows, :] = (y * g[rows]).astype(BF16)


def _out_kernel(x_ref, oa_ref, ob_ref, w_ref, gf_ref, y_ref, *, a_width, final):
    y = x_ref[0] + _dot_tn(oa_ref[0, 0], w_ref[0:a_width, :]) + _dot_tn(ob_ref[0, 0], w_ref[a_width:, :])
    if final:
        y = y * lax.rsqrt(jnp.mean(y * y, axis=-1, keepdims=True) + EPS) * gf_ref[...]
    y_ref[0] = y


def _rope_tables(seq):
    def cs(pos, dim):
        inv = ROPE_THETA ** (-jnp.arange(0, dim, 2, dtype=F32) / dim)
        ang = pos[None, :] * inv[:, None]
        return jnp.cos(ang), jnp.sin(ang)

    rows = seq // GRID_W
    row = jnp.repeat(jnp.arange(rows, dtype=F32), GRID_W)
    col = jnp.tile(jnp.arange(GRID_W, dtype=F32), rows)
    pos = jnp.arange(seq, dtype=F32)
    cr, sr = cs(row, HEAD_DIM // 2)
    cc, sc = cs(col, HEAD_DIM // 2)
    cp, sp = cs(pos, HEAD_DIM)
    ca = jnp.concatenate([cr, cr, cc, cc], axis=0)
    sa = jnp.concatenate([-sr, sr, -sc, sc], axis=0)
    cb = jnp.concatenate([cp, cp], axis=0)
    sb = jnp.concatenate([-sp, sp], axis=0)
    return ca, sa, cb, sb


def _layer(x, gn, wt, qn, kn, lam_vecs, subln, w_out, g_final, lam_init, final):
    bsz, seq, d = x.shape
    a_width = (d // (2 * HEAD_DIM)) * HEAD_DIM
    a_kv = max(1, (a_width // HEAD_DIM) // 4) * HEAD_DIM
    b_qk = (d // (4 * HEAD_DIM)) * 2 * HEAD_DIM
    b_width = b_qk
    assert a_width == HEADS_PER_STEP * a_kv and a_width % STEP_ROWS == 0 and b_width % STEP_ROWS == 0
    t = TOKEN_TILE
    assert seq % t == 0 and seq % GRID_W == 0
    nc = seq // t
    va_rows = (a_kv // HEAD_DIM) * (HEAD_DIM + BF16_SUBLANES)
    vb_rows = (b_width // (2 * HEAD_DIM)) * (2 * HEAD_DIM + BF16_SUBLANES)
    ca, sa, cb, sb = _rope_tables(seq)
    cparams = functools.partial(pltpu.CompilerParams, vmem_limit_bytes=VMEM_LIMIT_BYTES)

    def fm(rows):
        return jax.ShapeDtypeStruct((bsz, nc, rows, t), BF16)

    fm_spec = lambda rows: pl.BlockSpec((1, 1, rows, t), lambda b, i: (b, i, 0, 0))
    tab_spec = pl.BlockSpec((HEAD_DIM, t), lambda b, i: (0, i))
    full = lambda shape: pl.BlockSpec(shape, lambda b, i: (0,) * len(shape))

    qa, va, ga, qb, vb, gb, ka, kb = pl.pallas_call(
        functools.partial(_proj_kernel, sizes=(a_width, a_kv, b_qk, b_width)),
        grid=(bsz, nc),
        in_specs=[pl.BlockSpec((1, t, d), lambda b, i: (b, i, 0)), full((1, d)), full(wt.shape),
                  full((HEAD_DIM, 1)), full((HEAD_DIM, 1)), tab_spec, tab_spec, tab_spec, tab_spec],
        out_specs=[fm_spec(a_width), fm_spec(va_rows), fm_spec(a_width), fm_spec(b_qk), fm_spec(vb_rows),
                   fm_spec(b_width),
                   pl.BlockSpec((1, t, a_kv), lambda b, i: (b, i, 0)),
                   pl.BlockSpec((1, t, b_qk), lambda b, i: (b, i, 0))],
        out_shape=[fm(a_width), fm(va_rows), fm(a_width), fm(b_qk), fm(vb_rows), fm(b_width),
                   jax.ShapeDtypeStruct((bsz, seq, a_kv), BF16),
                   jax.ShapeDtypeStruct((bsz, seq, b_qk), BF16)],
        compiler_params=cparams(dimension_semantics=("parallel", "parallel")),
        name="proj",
    )(x, gn, wt, qn, kn, ca, sa, cb, sb)

    group = max(1, min(HEADS_PER_STEP, PHASE_PAIRS // nc))
    nq = 2 * group // HEADS_PER_STEP
    assert nq >= 1 and nc % nq == 0
    grid_a = (bsz, a_width // STEP_ROWS, nc // nq)
    grid_b = (bsz, b_width // STEP_ROWS, nc // nq)

    def next_step(dims):
        nb, nu, ni = dims

        def f(b, u, i):
            cu = (i + 1) // ni
            cb = (u + cu) // nu
            return jnp.minimum(b + cb, nb - 1), (u + cu) % nu, (i + 1) % ni
        return f

    step_spec = pl.BlockSpec((1, nq, STEP_ROWS, t), lambda b, u, i: (b, i, u, 0))
    sem = ("arbitrary", "arbitrary", "arbitrary")

    def next_q_spec(dims):
        nxt = next_step(dims)
        rows = group * HEAD_DIM

        def index(b, u, i):
            b2, u2, i2 = nxt(b, u, i)
            return b2, i2 * nq, u2 * (STEP_ROWS // rows), 0
        return pl.BlockSpec((1, 1, rows, t), index)

    def scratch(dve):
        return ([pltpu.VMEM((seq, t), F32)] * (2 * group)
                + [pltpu.VMEM((nq * HEADS_PER_STEP, dve, t), F32), pltpu.VMEM((group, 8, t), F32)])

    dve_a = HEAD_DIM + BF16_SUBLANES
    ka_spec = lambda index: pl.BlockSpec((1, seq, a_kv), index)
    oa = pl.pallas_call(
        functools.partial(_attn_kernel, mode="A", n_chunks=nc, lam_init=lam_init, grid_dims=grid_a, group=group),
        grid=grid_a,
        in_specs=[step_spec, next_q_spec(grid_a),
                  ka_spec(lambda b, u, i: (b, 0, 0)),
                  ka_spec(lambda b, u, i: (next_step(grid_a)(b, u, i)[0], 0, 0)),
                  pl.BlockSpec((1, nc, dve_a, t), lambda b, u, i: (b, 0, u, 0)),
                  step_spec],
        out_specs=step_spec,
        out_shape=fm(a_width),
        scratch_shapes=scratch(dve_a),
        compiler_params=cparams(dimension_semantics=sem),
        name="attn_a",
    )(qa, qa, ka, ka, va, ga)

    dve_b = 2 * HEAD_DIM + BF16_SUBLANES
    kb_spec = lambda index: pl.BlockSpec((1, seq, STEP_ROWS), index)

    def kb_next(b, u, i):
        b2, u2, _ = next_step(grid_b)(b, u, i)
        return b2, 0, u2

    ob = pl.pallas_call(
        functools.partial(_attn_kernel, mode="B", n_chunks=nc, lam_init=lam_init, grid_dims=grid_b, group=group),
        grid=grid_b,
        in_specs=[step_spec, next_q_spec(grid_b),
                  kb_spec(lambda b, u, i: (b, 0, u)),
                  kb_spec(kb_next),
                  pl.BlockSpec((1, nc, 2 * dve_b, t), lambda b, u, i: (b, 0, u, 0)),
                  step_spec,
                  pl.BlockSpec((4, HEAD_DIM), lambda b, u, i: (0, 0)),
                  pl.BlockSpec((2 * HEAD_DIM, 1), lambda b, u, i: (0, 0))],
        out_specs=step_spec,
        out_shape=fm(b_width),
        scratch_shapes=scratch(dve_b),
        compiler_params=cparams(dimension_semantics=sem),
        name="attn_b",
    )(qb, qb, kb, kb, vb, gb, lam_vecs, subln)

    return pl.pallas_call(
        functools.partial(_out_kernel, a_width=a_width, final=final),
        grid=(bsz, nc),
        in_specs=[pl.BlockSpec((1, t, d), lambda b, i: (b, i, 0)), fm_spec(a_width), fm_spec(b_width),
                  full(w_out.shape), full((1, d))],
        out_specs=pl.BlockSpec((1, t, d), lambda b, i: (b, i, 0)),
        out_shape=jax.ShapeDtypeStruct(x.shape, x.dtype),
        compiler_params=cparams(dimension_semantics=("parallel", "parallel")),
        name="out_proj",
    )(x, oa, ob, w_out, g_final)


def kernel(x_prompt, x_sample, g_norm, w_in, a_q_norm, a_k_norm, b_lambda_q1, b_lambda_k1,
           b_lambda_q2, b_lambda_k2, b_subln, w_out, g_final):
    depth = w_in.shape[0]
    hp, hs = x_prompt, x_sample
    for l in range(depth):
        lam_init = 0.8 - 0.6 * math.exp(-0.3 * l)
        args = (g_norm[l][None, :], w_in[l].T.astype(BF16), a_q_norm[l][:, None], a_k_norm[l][:, None],
                jnp.stack([b_lambda_q1[l], b_lambda_k1[l], b_lambda_q2[l], b_lambda_k2[l]]).astype(F32),
                b_subln[l][:, None], w_out[l].astype(BF16), g_final[None, :])
        final = l == depth - 1
        hp = _layer(hp, *args, lam_init, final)
        hs = _layer(hs, *args, lam_init, final)
    return (hp, hs)
```

```python
import functools
import math

import jax
import jax.numpy as jnp
from jax import lax
from jax.experimental import pallas as pl
from jax.experimental.pallas import tpu as pltpu

HEAD_DIM = 64
GRID_W = 64
ROPE_THETA = 10000.0
EPS = 1e-6
TOKEN_TILE = 512
SUB_KEYS = 512
HEADS_PER_STEP = 4
STEP_ROWS = HEADS_PER_STEP * HEAD_DIM
PHASE_PAIRS = 16
BF16_SUBLANES = 16
VMEM_LIMIT_BYTES = 56 * 1024 * 1024
LOG2E = math.log2(math.e)

F32 = jnp.float32
BF16 = jnp.bfloat16


def _dot_nt(a, b):
    return lax.dot_general(a, b, (((1,), (1,)), ((), ())), preferred_element_type=F32)


def _dot_tn(a, b):
    return lax.dot_general(a, b, (((0,), (0,)), ((), ())), preferred_element_type=F32)


def _dot(a, b):
    return jnp.dot(a, b, preferred_element_type=F32)


def _rope_axial(x, c, s):
    xs = jnp.concatenate([x[16:32], x[0:16], x[48:64], x[32:48]], axis=0)
    return x * c + xs * s


def _rope_1d(x, c, s):
    xs = jnp.concatenate([x[32:64], x[0:32]], axis=0)
    return x * c + xs * s


def _silu(g):
    return g * (1.0 / (1.0 + jnp.exp(-g)))


def _proj_kernel(x_ref, gn_ref, wt_ref, qn_ref, kn_ref, ca_ref, sa_ref, cb_ref, sb_ref,
                 qa_ref, va_ref, ga_ref, qb_ref, vb_ref, gb_ref, ka_ref, kb_ref, *, sizes):
    a_width, a_kv, b_qk, b_width = sizes
    x = x_ref[0]
    t = x.shape[0]
    h = x * lax.rsqrt(jnp.mean(x * x, axis=-1, keepdims=True) + EPS) * gn_ref[...]
    hb = h.astype(BF16)

    def section(lo, n):
        return _dot_nt(wt_ref[lo:lo + n, :], hb)

    ca, sa, cb, sb = ca_ref[...], sa_ref[...], cb_ref[...], sb_ref[...]
    qscale = HEAD_DIM ** -0.5 * LOG2E
    ones_rows = (lax.broadcasted_iota(jnp.int32, (BF16_SUBLANES, t), 0) == 0).astype(BF16)

    def head_norm(blk, g):
        return blk * lax.rsqrt(jnp.mean(blk * blk, axis=0, keepdims=True) + EPS) * g

    def store_values(v_ref, vals, dv):
        dve = dv + BF16_SUBLANES
        for hd in range(vals.shape[0] // dv):
            v_ref[0, 0, hd * dve:hd * dve + dv, :] = vals[hd * dv:(hd + 1) * dv].astype(BF16)
            v_ref[0, 0, hd * dve + dv:(hd + 1) * dve, :] = ones_rows

    off = 0
    r = section(off, a_width)
    for hd in range(a_width // HEAD_DIM):
        blk = r[hd * HEAD_DIM:(hd + 1) * HEAD_DIM]
        y = _rope_axial(head_norm(blk, qn_ref[...]), ca, sa) * qscale
        qa_ref[0, 0, hd * HEAD_DIM:(hd + 1) * HEAD_DIM, :] = y.astype(BF16)
    off += a_width

    r = section(off, 2 * a_kv)
    ks = []
    for hd in range(a_kv // HEAD_DIM):
        blk = r[hd * HEAD_DIM:(hd + 1) * HEAD_DIM]
        ks.append(_rope_axial(head_norm(blk, kn_ref[...]), ca, sa))
    ka_ref[0] = jnp.concatenate(ks, axis=0).T.astype(BF16)
    store_values(va_ref, r[a_kv:2 * a_kv], HEAD_DIM)
    off += 2 * a_kv

    ga_ref[0, 0] = _silu(section(off, a_width)).astype(BF16)
    off += a_width

    r = section(off, b_qk)
    for hd in range(b_qk // HEAD_DIM):
        blk = r[hd * HEAD_DIM:(hd + 1) * HEAD_DIM]
        qb_ref[0, 0, hd * HEAD_DIM:(hd + 1) * HEAD_DIM, :] = (_rope_1d(blk, cb, sb) * qscale).astype(BF16)
    off += b_qk

    r = section(off, b_qk)
    ks = [_rope_1d(r[hd * HEAD_DIM:(hd + 1) * HEAD_DIM], cb, sb) for hd in range(b_qk // HEAD_DIM)]
    kb_ref[0] = jnp.concatenate(ks, axis=0).T.astype(BF16)
    off += b_qk

    store_values(vb_ref, section(off, b_width), 2 * HEAD_DIM)
    off += b_width
    gb_ref[0, 0] = _silu(section(off, b_width)).astype(BF16)


def _attn_kernel(*refs, mode, n_chunks, lam_init, grid_dims, group):
    n_s = 2 * group
    if mode == "A":
        q_ref, qn_ref, k_ref, kn_ref, v_ref, g_ref, o_ref = refs[:7]
        dv = HEAD_DIM
    else:
        q_ref, qn_ref, k_ref, kn_ref, v_ref, g_ref, lam_ref, sub_ref, o_ref = refs[:9]
        dv = 2 * HEAD_DIM
    s_refs, (acc_ref, mcar_ref) = refs[-2 - n_s:-2], refs[-2:]
    dve = dv + BF16_SUBLANES
    nq, tq, tk = q_ref.shape[1], q_ref.shape[-1], v_ref.shape[-1]
    n_maps = nq * HEADS_PER_STEP
    assert n_maps == 2 * group
    slots = k_ref.shape[-1] // HEAD_DIM
    n_sub = tk // SUB_KEYS
    _, nu, ni = grid_dims
    b, u, i = pl.program_id(0), pl.program_id(1), pl.program_id(2)
    is_first = (b == 0) & (u == 0) & (i == 0)
    u_next = (u + (i + 1) // ni) % nu
    z = jnp.zeros((HEAD_DIM, tq), BF16)

    def place(rows64, slot):
        rows = [z] * slots
        rows[slot] = rows64
        return jnp.concatenate(rows, axis=0)

    def expand(rows64, head, unit):
        if mode == "A":
            return jnp.where(unit == 0, place(rows64, 0), place(rows64, 1))
        return place(rows64, head)

    def head_rows(m):
        hd = m % HEADS_PER_STEP
        return slice(hd * HEAD_DIM, (hd + 1) * HEAD_DIM)

    qx = [expand(q_ref[0, m // HEADS_PER_STEP, head_rows(m), :], m % HEADS_PER_STEP, u) for m in range(n_maps)]
    qx_next = [expand(qn_ref[0, 0, head_rows(m), :], m, u_next) for m in range(group)]
    v_rows = [slice(0, dve) if mode == "A" else slice(((m % HEADS_PER_STEP) // 2) * dve,
                                                      ((m % HEADS_PER_STEP) // 2 + 1) * dve)
              for m in range(n_maps)]

    def keys(c, j):
        return pl.ds(c * tk + j * SUB_KEYS, SUB_KEYS)

    def pass1(kref, qxm, s_dst, c, j, m8):
        s = _dot(kref[0, keys(c, j), :], qxm)
        s_dst[keys(c, j), :] = s
        return jnp.maximum(m8, jnp.max(s.reshape(SUB_KEYS // 8, 8, tq), axis=0))

    def pass2(m, s_src, mb, c, j):
        s = s_src[keys(c, j), :]
        p = jnp.exp2(s.reshape(SUB_KEYS // 8, 8, tq) - mb[None]).reshape(SUB_KEYS, tq).astype(BF16)
        vs = v_ref[0, c, v_rows[m], j * SUB_KEYS:(j + 1) * SUB_KEYS]
        return _dot(vs, p)

    neg = jnp.full((8, tq), -jnp.inf, F32)

    def run_phase(trips, p1s, p2s):
        def phase(_, m8s):
            m8s = list(m8s)
            for c in range(n_chunks):
                parts = [None] * len(p2s)
                for j in range(n_sub):
                    for k in range(max(len(p1s), len(p2s))):
                        if k < len(p1s):
                            m8s[k] = pass1(*p1s[k], c, j, m8s[k])
                        if k < len(p2s):
                            d = pass2(*p2s[k], c, j)
                            parts[k] = d if parts[k] is None else parts[k] + d
                for k, p2 in enumerate(p2s):
                    acc_ref[p2[0]] += parts[k]
            return tuple(m8s)

        return lax.fori_loop(0, trips, phase, (neg,) * len(p1s))

    def col_max(m8):
        return jnp.broadcast_to(jnp.max(m8, axis=0, keepdims=True), (8, tq))

    def s_buf(parity, k):
        return s_refs[parity * group + k]

    acc_ref[...] = jnp.zeros(acc_ref.shape, F32)
    m_first = run_phase(is_first.astype(jnp.int32), [(k_ref, qx[k], s_buf(0, k)) for k in range(group)], [])

    @pl.when(is_first)
    def _():
        for k in range(group):
            mcar_ref[k] = m_first[k]

    once = jnp.minimum(i, 0) + 1
    m8s = [mcar_ref[k] for k in range(group)]
    m8s = run_phase(once, [(k_ref, qx[group + k], s_buf(1, k)) for k in range(group)],
                    [(k, s_buf(0, k), col_max(m8s[k])) for k in range(group)])
    m8s = run_phase(once, [(kn_ref, qx_next[k], s_buf(0, k)) for k in range(group)],
                    [(group + k, s_buf(1, k), col_max(m8s[k])) for k in range(group)])
    for k in range(group):
        mcar_ref[k] = m8s[k]

    def normalised(m):
        acc = acc_ref[m]
        return acc[0:dv] * (1.0 / acc[dv:dv + 1])

    if mode == "B":
        lv = lam_ref[...]
        lam = (jnp.exp(jnp.sum(lv[0:1] * lv[1:2], axis=-1, keepdims=True))
               - jnp.exp(jnp.sum(lv[2:3] * lv[3:4], axis=-1, keepdims=True)) + lam_init)
    for t_idx in range(nq):
        g = g_ref[0, t_idx].astype(F32)
        base = t_idx * HEADS_PER_STEP
        if mode == "A":
            for hd in range(HEADS_PER_STEP):
                rows = slice(hd * dv, (hd + 1) * dv)
                o_ref[0, t_idx, rows, :] = (normalised(base + hd) * g[rows]).astype(BF16)
        else:
            for j in range(HEADS_PER_STEP // 2):
                o = normalised(base + 2 * j) - lam * normalised(base + 2 * j + 1)
                y = o * lax.rsqrt(jnp.mean(o * o, axis=0, keepdims=True) + EPS) * sub_ref[...]
                y = y * (1.0 - lam_init)
                rows = slice(j * dv, (j + 1) * dv)
                o_ref[0, t_idx, rows, :] = (y * g[rows]).astype(BF16)


def _out_kernel(x_ref, oa_ref, ob_ref, w_ref, gf_ref, y_ref, *, a_width, final):
    y = x_ref[0] + _dot_tn(oa_ref[0, 0], w_ref[0:a_width, :]) + _dot_tn(ob_ref[0, 0], w_ref[a_width:, :])
    if final:
        y = y * lax.rsqrt(jnp.mean(y * y, axis=-1, keepdims=True) + EPS) * gf_ref[...]
    y_ref[0] = y


def _rope_tables(seq):
    def cs(pos, dim):
        inv = ROPE_THETA ** (-jnp.arange(0, dim, 2, dtype=F32) / dim)
        ang = pos[None, :] * inv[:, None]
        return jnp.cos(ang), jnp.sin(ang)

    rows = seq // GRID_W
    row = jnp.repeat(jnp.arange(rows, dtype=F32), GRID_W)
    col = jnp.tile(jnp.arange(GRID_W, dtype=F32), rows)
    pos = jnp.arange(seq, dtype=F32)
    cr, sr = cs(row, HEAD_DIM // 2)
    cc, sc = cs(col, HEAD_DIM // 2)
    cp, sp = cs(pos, HEAD_DIM)
    ca = jnp.concatenate([cr, cr, cc, cc], axis=0)
    sa = jnp.concatenate([-sr, sr, -sc, sc], axis=0)
    cb = jnp.concatenate([cp, cp], axis=0)
    sb = jnp.concatenate([-sp, sp], axis=0)
    return ca, sa, cb, sb


def _layer(x, gn, wt, qn, kn, lam_vecs, subln, w_out, g_final, lam_init, final):
    bsz, seq, d = x.shape
    a_width = (d // (2 * HEAD_DIM)) * HEAD_DIM
    a_kv = max(1, (a_width // HEAD_DIM) // 4) * HEAD_DIM
    b_qk = (d // (4 * HEAD_DIM)) * 2 * HEAD_DIM
    b_width = b_qk
    assert a_width == HEADS_PER_STEP * a_kv and a_width % STEP_ROWS == 0 and b_width % STEP_ROWS == 0
    t = TOKEN_TILE
    assert seq % t == 0 and seq % GRID_W == 0
    nc = seq // t
    va_rows = (a_kv // HEAD_DIM) * (HEAD_DIM + BF16_SUBLANES)
    vb_rows = (b_width // (2 * HEAD_DIM)) * (2 * HEAD_DIM + BF16_SUBLANES)
    ca, sa, cb, sb = _rope_tables(seq)
    cparams = functools.partial(pltpu.CompilerParams, vmem_limit_bytes=VMEM_LIMIT_BYTES)

    def fm(rows):
        return jax.ShapeDtypeStruct((bsz, nc, rows, t), BF16)

    fm_spec = lambda rows: pl.BlockSpec((1, 1, rows, t), lambda b, i: (b, i, 0, 0))
    tab_spec = pl.BlockSpec((HEAD_DIM, t), lambda b, i: (0, i))
    full = lambda shape: pl.BlockSpec(shape, lambda b, i: (0,) * len(shape))

    qa, va, ga, qb, vb, gb, ka, kb = pl.pallas_call(
        functools.partial(_proj_kernel, sizes=(a_width, a_kv, b_qk, b_width)),
        grid=(bsz, nc),
        in_specs=[pl.BlockSpec((1, t, d), lambda b, i: (b, i, 0)), full((1, d)), full(wt.shape),
                  full((HEAD_DIM, 1)), full((HEAD_DIM, 1)), tab_spec, tab_spec, tab_spec, tab_spec],
        out_specs=[fm_spec(a_width), fm_spec(va_rows), fm_spec(a_width), fm_spec(b_qk), fm_spec(vb_rows),
                   fm_spec(b_width),
                   pl.BlockSpec((1, t, a_kv), lambda b, i: (b, i, 0)),
                   pl.BlockSpec((1, t, b_qk), lambda b, i: (b, i, 0))],
        out_shape=[fm(a_width), fm(va_rows), fm(a_width), fm(b_qk), fm(vb_rows), fm(b_width),
                   jax.ShapeDtypeStruct((bsz, seq, a_kv), BF16),
                   jax.ShapeDtypeStruct((bsz, seq, b_qk), BF16)],
        compiler_params=cparams(dimension_semantics=("parallel", "parallel")),
        name="proj",
    )(x, gn, wt, qn, kn, ca, sa, cb, sb)

    group = max(1, min(HEADS_PER_STEP, PHASE_PAIRS // nc))
    nq = 2 * group // HEADS_PER_STEP
    assert nq >= 1 and nc % nq == 0
    grid_a = (bsz, a_width // STEP_ROWS, nc // nq)
    grid_b = (bsz, b_width // STEP_ROWS, nc // nq)

    def next_step(dims):
        nb, nu, ni = dims

        def f(b, u, i):
            cu = (i + 1) // ni
            cb = (u + cu) // nu
            return jnp.minimum(b + cb, nb - 1), (u + cu) % nu, (i + 1) % ni
        return f

    step_spec = pl.BlockSpec((1, nq, STEP_ROWS, t), lambda b, u, i: (b, i, u, 0))
    sem = ("arbitrary", "arbitrary", "arbitrary")

    def next_q_spec(dims):
        nxt = next_step(dims)
        rows = group * HEAD_DIM

        def index(b, u, i):
            b2, u2, i2 = nxt(b, u, i)
            return b2, i2 * nq, u2 * (STEP_ROWS // rows), 0
        return pl.BlockSpec((1, 1, rows, t), index)

    def scratch(dve):
        return ([pltpu.VMEM((seq, t), F32)] * (2 * group)
                + [pltpu.VMEM((nq * HEADS_PER_STEP, dve, t), F32), pltpu.VMEM((group, 8, t), F32)])

    dve_a = HEAD_DIM + BF16_SUBLANES
    ka_spec = lambda index: pl.BlockSpec((1, seq, a_kv), index)
    oa = pl.pallas_call(
        functools.partial(_attn_kernel, mode="A", n_chunks=nc, lam_init=lam_init, grid_dims=grid_a, group=group),
        grid=grid_a,
        in_specs=[step_spec, next_q_spec(grid_a),
                  ka_spec(lambda b, u, i: (b, 0, 0)),
                  ka_spec(lambda b, u, i: (next_step(grid_a)(b, u, i)[0], 0, 0)),
                  pl.BlockSpec((1, nc, dve_a, t), lambda b, u, i: (b, 0, u, 0)),
                  step_spec],
        out_specs=step_spec,
        out_shape=fm(a_width),
        scratch_shapes=scratch(dve_a),
        compiler_params=cparams(dimension_semantics=sem),
        name="attn_a",
    )(qa, qa, ka, ka, va, ga)

    dve_b = 2 * HEAD_DIM + BF16_SUBLANES
    kb_spec = lambda index: pl.BlockSpec((1, seq, STEP_ROWS), index)

    def kb_next(b, u, i):
        b2, u2, _ = next_step(grid_b)(b, u, i)
        return b2, 0, u2

    ob = pl.pallas_call(
        functools.partial(_attn_kernel, mode="B", n_chunks=nc, lam_init=lam_init, grid_dims=grid_b, group=group),
        grid=grid_b,
        in_specs=[step_spec, next_q_spec(grid_b),
                  kb_spec(lambda b, u, i: (b, 0, u)),
                  kb_spec(kb_next),
                  pl.BlockSpec((1, nc, 2 * dve_b, t), lambda b, u, i: (b, 0, u, 0)),
                  step_spec,
                  pl.BlockSpec((4, HEAD_DIM), lambda b, u, i: (0, 0)),
                  pl.BlockSpec((2 * HEAD_DIM, 1), lambda b, u, i: (0, 0))],
        out_specs=step_spec,
        out_shape=fm(b_width),
        scratch_shapes=scratch(dve_b),
        compiler_params=cparams(dimension_semantics=sem),
        name="attn_b",
    )(qb, qb, kb, kb, vb, gb, lam_vecs, subln)

    return pl.pallas_call(
        functools.partial(_out_kernel, a_width=a_width, final=final),
        grid=(bsz, nc),
        in_specs=[pl.BlockSpec((1, t, d), lambda b, i: (b, i, 0)), fm_spec(a_width), fm_spec(b_width),
                  full(w_out.shape), full((1, d))],
        out_specs=pl.BlockSpec((1, t, d), lambda b, i: (b, i, 0)),
        out_shape=jax.ShapeDtypeStruct(x.shape, x.dtype),
        compiler_params=cparams(dimension_semantics=("parallel", "parallel")),
        name="out_proj",
    )(x, oa, ob, w_out, g_final)


def kernel(x_prompt, x_sample, g_norm, w_in, a_q_norm, a_k_norm, b_lambda_q1, b_lambda_k1,
           b_lambda_q2, b_lambda_k2, b_subln, w_out, g_final):
    depth = w_in.shape[0]
    hp, hs = x_prompt, x_sample
    for l in range(depth):
        lam_init = 0.8 - 0.6 * math.exp(-0.3 * l)
        args = (g_norm[l][None, :], w_in[l].T.astype(BF16), a_q_norm[l][:, None], a_k_norm[l][:, None],
                jnp.stack([b_lambda_q1[l], b_lambda_k1[l], b_lambda_q2[l], b_lambda_k2[l]]).astype(F32),
                b_subln[l][:, None], w_out[l].astype(BF16), g_final[None, :])
        final = l == depth - 1
        hp = _layer(hp, *args, lam_init, final)
        hs = _layer(hs, *args, lam_init, final)
    return (hp, hs)
```

```python
import functools
import math

import jax
import jax.numpy as jnp
from jax import lax
from jax.experimental import pallas as pl
from jax.experimental.pallas import tpu as pltpu

HEAD_DIM = 64
GRID_W = 64
ROPE_THETA = 10000.0
EPS = 1e-6
TOKEN_TILE = 512
SUB_KEYS = 256
HEADS_PER_STEP = 4
STEP_ROWS = HEADS_PER_STEP * HEAD_DIM
PHASE_PAIRS = 16
PROJ_TILES = 2
BF16_SUBLANES = 16
LANES = 128
VMEM_LIMIT_BYTES = 56 * 1024 * 1024
LOG2E = math.log2(math.e)

F32 = jnp.float32
BF16 = jnp.bfloat16


def _dot_nt(a, b):
    return lax.dot_general(a, b, (((1,), (1,)), ((), ())), preferred_element_type=F32)


def _dot_tn(a, b):
    return lax.dot_general(a, b, (((0,), (0,)), ((), ())), preferred_element_type=F32)


def _dot(a, b):
    return jnp.dot(a, b, preferred_element_type=F32)


def _rope_axial(x, c, s):
    xs = jnp.concatenate([x[16:32], x[0:16], x[48:64], x[32:48]], axis=0)
    return x * c + xs * s


def _rope_1d(x, c, s):
    xs = jnp.concatenate([x[32:64], x[0:32]], axis=0)
    return x * c + xs * s


def _silu(g):
    return g * (1.0 / (1.0 + jnp.exp(-g)))


def _proj_kernel(x_ref, gn_ref, wt_ref, qn_ref, kn_ref, ca_ref, sa_ref, cb_ref, sb_ref,
                 qa_ref, va_ref, ga_ref, qb_ref, vb_ref, gb_ref, ka_ref, kb_ref, *, sizes):
    n_tiles, t = qa_ref.shape[1], qa_ref.shape[-1]
    for j in range(n_tiles):
        _proj_tile(j, slice(j * t, (j + 1) * t), x_ref, gn_ref, wt_ref, qn_ref, kn_ref, ca_ref, sa_ref, cb_ref,
                   sb_ref, qa_ref, va_ref, ga_ref, qb_ref, vb_ref, gb_ref, ka_ref, kb_ref, sizes)


def _proj_tile(j, tok, x_ref, gn_ref, wt_ref, qn_ref, kn_ref, ca_ref, sa_ref, cb_ref, sb_ref,
               qa_ref, va_ref, ga_ref, qb_ref, vb_ref, gb_ref, ka_ref, kb_ref, sizes):
    a_width, a_kv, b_qk, b_width = sizes
    x = x_ref[0, tok, :]
    t = x.shape[0]
    h = x * lax.rsqrt(jnp.mean(x * x, axis=-1, keepdims=True) + EPS) * gn_ref[...]
    hb = h.astype(BF16)

    def section(lo, n):
        return _dot_nt(wt_ref[lo:lo + n, :], hb)

    ca, sa, cb, sb = ca_ref[:, tok], sa_ref[:, tok], cb_ref[:, tok], sb_ref[:, tok]
    qscale = HEAD_DIM ** -0.5 * LOG2E
    ones_rows = (lax.broadcasted_iota(jnp.int32, (BF16_SUBLANES, t), 0) == 0).astype(BF16)

    def head_norm(blk, g):
        return blk * lax.rsqrt(jnp.mean(blk * blk, axis=0, keepdims=True) + EPS) * g

    def store_values(v_ref, vals, dv):
        dve = dv + BF16_SUBLANES
        for hd in range(vals.shape[0] // dv):
            v_ref[0, j, hd * dve:hd * dve + dv, :] = vals[hd * dv:(hd + 1) * dv].astype(BF16)
            v_ref[0, j, hd * dve + dv:(hd + 1) * dve, :] = ones_rows

    off = 0
    r = section(off, a_width)
    for hd in range(a_width // HEAD_DIM):
        blk = r[hd * HEAD_DIM:(hd + 1) * HEAD_DIM]
        y = _rope_axial(head_norm(blk, qn_ref[...]), ca, sa) * qscale
        qa_ref[0, j, hd * HEAD_DIM:(hd + 1) * HEAD_DIM, :] = y.astype(BF16)
    off += a_width

    r = section(off, 2 * a_kv)
    ks = []
    for hd in range(a_kv // HEAD_DIM):
        blk = r[hd * HEAD_DIM:(hd + 1) * HEAD_DIM]
        ks.append(_rope_axial(head_norm(blk, kn_ref[...]), ca, sa))
    ka_ref[0, tok, :] = jnp.concatenate(ks, axis=0).T.astype(BF16)
    store_values(va_ref, r[a_kv:2 * a_kv], HEAD_DIM)
    off += 2 * a_kv

    ga_ref[0, j] = _silu(section(off, a_width)).astype(BF16)
    off += a_width

    r = section(off, b_qk)
    for hd in range(b_qk // HEAD_DIM):
        blk = r[hd * HEAD_DIM:(hd + 1) * HEAD_DIM]
        qb_ref[0, j, hd * HEAD_DIM:(hd + 1) * HEAD_DIM, :] = (_rope_1d(blk, cb, sb) * qscale).astype(BF16)
    off += b_qk

    r = section(off, b_qk)
    ks = [_rope_1d(r[hd * HEAD_DIM:(hd + 1) * HEAD_DIM], cb, sb) for hd in range(b_qk // HEAD_DIM)]
    kb_ref[0, tok, :] = jnp.concatenate(ks, axis=0).T.astype(BF16)
    off += b_qk

    store_values(vb_ref, section(off, b_width), 2 * HEAD_DIM)
    off += b_width
    gb_ref[0, j] = _silu(section(off, b_width)).astype(BF16)


def _attn_kernel(*refs, mode, n_chunks, lam_init, grid_dims, group):
    n_s = 2 * group
    if mode == "A":
        q_ref, qn_ref, k_ref, kn_ref, v_ref, g_ref, o_ref = refs[:7]
        dv = HEAD_DIM
    else:
        q_ref, qn_ref, k_ref, kn_ref, v_ref, g_ref, lam_ref, sub_ref, o_ref = refs[:9]
        dv = 2 * HEAD_DIM
    s_refs, (acc_ref, mcar_ref) = refs[-2 - n_s:-2], refs[-2:]
    dve = dv + BF16_SUBLANES
    nq, tq, tk = q_ref.shape[1], q_ref.shape[-1], v_ref.shape[-1]
    n_maps = nq * HEADS_PER_STEP
    assert n_maps == 2 * group
    slots = k_ref.shape[-1] // HEAD_DIM
    n_sub = tk // SUB_KEYS
    _, nu, ni = grid_dims
    b, u, i = pl.program_id(0), pl.program_id(1), pl.program_id(2)
    is_first = (b == 0) & (u == 0) & (i == 0)
    u_next = (u + (i + 1) // ni) % nu
    z = jnp.zeros((HEAD_DIM, tq), BF16)

    def place(rows64, slot):
        rows = [z] * slots
        rows[slot] = rows64
        return jnp.concatenate(rows, axis=0)

    def expand(rows64, head, unit):
        if mode == "A":
            return jnp.where(unit == 0, place(rows64, 0), place(rows64, 1))
        return place(rows64, head)

    def head_rows(m):
        hd = m % HEADS_PER_STEP
        return slice(hd * HEAD_DIM, (hd + 1) * HEAD_DIM)

    qx = [expand(q_ref[0, m // HEADS_PER_STEP, head_rows(m), :], m % HEADS_PER_STEP, u) for m in range(n_maps)]
    qx_next = [expand(qn_ref[0, 0, head_rows(m), :], m, u_next) for m in range(group)]
    v_rows = [slice(0, dve) if mode == "A" else slice(((m % HEADS_PER_STEP) // 2) * dve,
                                                      ((m % HEADS_PER_STEP) // 2 + 1) * dve)
              for m in range(n_maps)]

    def keys(c, j):
        return pl.ds(c * tk + j * SUB_KEYS, SUB_KEYS)

    def pass1(kref, qxm, s_dst, c, j, m8):
        s = _dot(kref[0, keys(c, j), :], qxm)
        s_dst[keys(c, j), :] = s
        return jnp.maximum(m8, jnp.max(s.reshape(SUB_KEYS // 8, 8, tq), axis=0))

    def pass2(m, s_src, mb, c, j):
        s = s_src[keys(c, j), :]
        p = jnp.exp2(s.reshape(SUB_KEYS // 8, 8, tq) - mb[None]).reshape(SUB_KEYS, tq).astype(BF16)
        vs = v_ref[0, c, v_rows[m], j * SUB_KEYS:(j + 1) * SUB_KEYS]
        return _dot(vs, p)

    neg = jnp.full((8, tq), -jnp.inf, F32)

    def run_phase(trips, p1s, p2s):
        def phase(_, m8s):
            m8s = list(m8s)
            for c in range(n_chunks):
                parts = [None] * len(p2s)
                for j in range(n_sub):
                    for k in range(max(len(p1s), len(p2s))):
                        if k < len(p1s):
                            m8s[k] = pass1(*p1s[k], c, j, m8s[k])
                        if k < len(p2s):
                            d = pass2(*p2s[k], c, j)
                            parts[k] = d if parts[k] is None else parts[k] + d
                for k, p2 in enumerate(p2s):
                    acc_ref[p2[0]] += parts[k]
            return tuple(m8s)

        return lax.fori_loop(0, trips, phase, (neg,) * len(p1s))

    def col_max(m8):
        return jnp.broadcast_to(jnp.max(m8, axis=0, keepdims=True), (8, tq))

    def s_buf(parity, k):
        return s_refs[parity * group + k]

    acc_ref[...] = jnp.zeros(acc_ref.shape, F32)
    m_first = run_phase(is_first.astype(jnp.int32), [(k_ref, qx[k], s_buf(0, k)) for k in range(group)], [])

    @pl.when(is_first)
    def _():
        for k in range(group):
            mcar_ref[k] = m_first[k]

    once = jnp.minimum(i, 0) + 1
    m8s = [mcar_ref[k] for k in range(group)]
    m8s = run_phase(once, [(k_ref, qx[group + k], s_buf(1, k)) for k in range(group)],
                    [(k, s_buf(0, k), col_max(m8s[k])) for k in range(group)])
    m8s = run_phase(once, [(kn_ref, qx_next[k], s_buf(0, k)) for k in range(group)],
                    [(group + k, s_buf(1, k), col_max(m8s[k])) for k in range(group)])
    for k in range(group):
        mcar_ref[k] = m8s[k]

    def normalised(m, lanes):
        return acc_ref[m, 0:dv, lanes] * (1.0 / acc_ref[m, dv:dv + 1, lanes])

    if mode == "B":
        lv = lam_ref[...]
        lam = (jnp.exp(jnp.sum(lv[0:1] * lv[1:2], axis=-1, keepdims=True))
               - jnp.exp(jnp.sum(lv[2:3] * lv[3:4], axis=-1, keepdims=True)) + lam_init)
        sub_gain = sub_ref[...] * (1.0 - lam_init)
    for t_idx in range(nq):
        base = t_idx * HEADS_PER_STEP
        for strip in range(tq // LANES):
            lanes = slice(strip * LANES, (strip + 1) * LANES)
            for j in range(STEP_ROWS // dv):
                rows = slice(j * dv, (j + 1) * dv)
                g = g_ref[0, t_idx, rows, lanes].astype(F32)
                if mode == "A":
                    y = normalised(base + j, lanes)
                else:
                    o = normalised(base + 2 * j, lanes) - lam * normalised(base + 2 * j + 1, lanes)
                    y = o * lax.rsqrt(jnp.mean(o * o, axis=0, keepdims=True) + EPS) * sub_gain
                o_ref[0, t_idx, rows, lanes] = (y * g).astype(BF16)


def _out_kernel(x_ref, oa_ref, ob_ref, w_ref, gf_ref, y_ref, *, a_width, final):
    n_tiles, t = oa_ref.shape[1], oa_ref.shape[-1]
    for j in range(n_tiles):
        tok = slice(j * t, (j + 1) * t)
        y = (x_ref[0, tok, :] + _dot_tn(oa_ref[0, j], w_ref[0:a_width, :])
             + _dot_tn(ob_ref[0, j], w_ref[a_width:, :]))
        if final:
            y = y * lax.rsqrt(jnp.mean(y * y, axis=-1, keepdims=True) + EPS) * gf_ref[...]
        y_ref[0, tok, :] = y


def _rope_tables(seq):
    def cs(pos, dim):
        inv = ROPE_THETA ** (-jnp.arange(0, dim, 2, dtype=F32) / dim)
        ang = pos[None, :] * inv[:, None]
        return jnp.cos(ang), jnp.sin(ang)

    rows = seq // GRID_W
    row = jnp.repeat(jnp.arange(rows, dtype=F32), GRID_W)
    col = jnp.tile(jnp.arange(GRID_W, dtype=F32), rows)
    pos = jnp.arange(seq, dtype=F32)
    cr, sr = cs(row, HEAD_DIM // 2)
    cc, sc = cs(col, HEAD_DIM // 2)
    cp, sp = cs(pos, HEAD_DIM)
    ca = jnp.concatenate([cr, cr, cc, cc], axis=0)
    sa = jnp.concatenate([-sr, sr, -sc, sc], axis=0)
    cb = jnp.concatenate([cp, cp], axis=0)
    sb = jnp.concatenate([-sp, sp], axis=0)
    return ca, sa, cb, sb


def _layer(x, gn, wt, qn, kn, lam_vecs, subln, w_out, g_final, lam_init, final):
    bsz, seq, d = x.shape
    a_width = (d // (2 * HEAD_DIM)) * HEAD_DIM
    a_kv = max(1, (a_width // HEAD_DIM) // 4) * HEAD_DIM
    b_qk = (d // (4 * HEAD_DIM)) * 2 * HEAD_DIM
    b_width = b_qk
    assert a_width == HEADS_PER_STEP * a_kv and a_width % STEP_ROWS == 0 and b_width % STEP_ROWS == 0
    t = TOKEN_TILE
    assert seq % t == 0 and seq % GRID_W == 0
    nc = seq // t
    va_rows = (a_kv // HEAD_DIM) * (HEAD_DIM + BF16_SUBLANES)
    vb_rows = (b_width // (2 * HEAD_DIM)) * (2 * HEAD_DIM + BF16_SUBLANES)
    ca, sa, cb, sb = _rope_tables(seq)
    cparams = functools.partial(pltpu.CompilerParams, vmem_limit_bytes=VMEM_LIMIT_BYTES)

    def fm(rows):
        return jax.ShapeDtypeStruct((bsz, nc, rows, t), BF16)

    pt = PROJ_TILES
    assert nc % pt == 0
    fm_spec = lambda rows: pl.BlockSpec((1, pt, rows, t), lambda b, i: (b, i, 0, 0))
    tab_spec = pl.BlockSpec((HEAD_DIM, pt * t), lambda b, i: (0, i))
    full = lambda shape: pl.BlockSpec(shape, lambda b, i: (0,) * len(shape))

    qa, va, ga, qb, vb, gb, ka, kb = pl.pallas_call(
        functools.partial(_proj_kernel, sizes=(a_width, a_kv, b_qk, b_width)),
        grid=(bsz, nc // pt),
        in_specs=[pl.BlockSpec((1, pt * t, d), lambda b, i: (b, i, 0)), full((1, d)), full(wt.shape),
                  full((HEAD_DIM, 1)), full((HEAD_DIM, 1)), tab_spec, tab_spec, tab_spec, tab_spec],
        out_specs=[fm_spec(a_width), fm_spec(va_rows), fm_spec(a_width), fm_spec(b_qk), fm_spec(vb_rows),
                   fm_spec(b_width),
                   pl.BlockSpec((1, pt * t, a_kv), lambda b, i: (b, i, 0)),
                   pl.BlockSpec((1, pt * t, b_qk), lambda b, i: (b, i, 0))],
        out_shape=[fm(a_width), fm(va_rows), fm(a_width), fm(b_qk), fm(vb_rows), fm(b_width),
                   jax.ShapeDtypeStruct((bsz, seq, a_kv), BF16),
                   jax.ShapeDtypeStruct((bsz, seq, b_qk), BF16)],
        compiler_params=cparams(dimension_semantics=("parallel", "parallel")),
        name="proj",
    )(x, gn, wt, qn, kn, ca, sa, cb, sb)

    group = max(1, min(HEADS_PER_STEP, PHASE_PAIRS // nc))
    nq = 2 * group // HEADS_PER_STEP
    assert nq >= 1 and nc % nq == 0
    grid_a = (bsz, a_width // STEP_ROWS, nc // nq)
    grid_b = (bsz, b_width // STEP_ROWS, nc // nq)

    def next_step(dims):
        nb, nu, ni = dims

        def f(b, u, i):
            cu = (i + 1) // ni
            cb = (u + cu) // nu
            return jnp.minimum(b + cb, nb - 1), (u + cu) % nu, (i + 1) % ni
        return f

    step_spec = pl.BlockSpec((1, nq, STEP_ROWS, t), lambda b, u, i: (b, i, u, 0))
    sem = ("arbitrary", "arbitrary", "arbitrary")

    def next_q_spec(dims):
        nxt = next_step(dims)
        rows = group * HEAD_DIM

        def index(b, u, i):
            b2, u2, i2 = nxt(b, u, i)
            return b2, i2 * nq, u2 * (STEP_ROWS // rows), 0
        return pl.BlockSpec((1, 1, rows, t), index)

    def scratch(dve):
        return ([pltpu.VMEM((seq, t), F32)] * (2 * group)
                + [pltpu.VMEM((nq * HEADS_PER_STEP, dve, t), F32), pltpu.VMEM((group, 8, t), F32)])

    dve_a = HEAD_DIM + BF16_SUBLANES
    ka_spec = lambda index: pl.BlockSpec((1, seq, a_kv), index)
    oa = pl.pallas_call(
        functools.partial(_attn_kernel, mode="A", n_chunks=nc, lam_init=lam_init, grid_dims=grid_a, group=group),
        grid=grid_a,
        in_specs=[step_spec, next_q_spec(grid_a),
                  ka_spec(lambda b, u, i: (b, 0, 0)),
                  ka_spec(lambda b, u, i: (next_step(grid_a)(b, u, i)[0], 0, 0)),
                  pl.BlockSpec((1, nc, dve_a, t), lambda b, u, i: (b, 0, u, 0)),
                  step_spec],
        out_specs=step_spec,
        out_shape=fm(a_width),
        scratch_shapes=scratch(dve_a),
        compiler_params=cparams(dimension_semantics=sem),
        name="attn_a",
    )(qa, qa, ka, ka, va, ga)

    dve_b = 2 * HEAD_DIM + BF16_SUBLANES
    kb_spec = lambda index: pl.BlockSpec((1, seq, STEP_ROWS), index)

    def kb_next(b, u, i):
        b2, u2, _ = next_step(grid_b)(b, u, i)
        return b2, 0, u2

    ob = pl.pallas_call(
        functools.partial(_attn_kernel, mode="B", n_chunks=nc, lam_init=lam_init, grid_dims=grid_b, group=group),
        grid=grid_b,
        in_specs=[step_spec, next_q_spec(grid_b),
                  kb_spec(lambda b, u, i: (b, 0, u)),
                  kb_spec(kb_next),
                  pl.BlockSpec((1, nc, 2 * dve_b, t), lambda b, u, i: (b, 0, u, 0)),
                  step_spec,
                  pl.BlockSpec((4, HEAD_DIM), lambda b, u, i: (0, 0)),
                  pl.BlockSpec((2 * HEAD_DIM, 1), lambda b, u, i: (0, 0))],
        out_specs=step_spec,
        out_shape=fm(b_width),
        scratch_shapes=scratch(dve_b),
        compiler_params=cparams(dimension_semantics=sem),
        name="attn_b",
    )(qb, qb, kb, kb, vb, gb, lam_vecs, subln)

    return pl.pallas_call(
        functools.partial(_out_kernel, a_width=a_width, final=final),
        grid=(bsz, nc // pt),
        in_specs=[pl.BlockSpec((1, pt * t, d), lambda b, i: (b, i, 0)), fm_spec(a_width), fm_spec(b_width),
                  full(w_out.shape), full((1, d))],
        out_specs=pl.BlockSpec((1, pt * t, d), lambda b, i: (b, i, 0)),
        out_shape=jax.ShapeDtypeStruct(x.shape, x.dtype),
        compiler_params=cparams(dimension_semantics=("parallel", "parallel")),
        name="out_proj",
    )(x, oa, ob, w_out, g_final)


def kernel(x_prompt, x_sample, g_norm, w_in, a_q_norm, a_k_norm, b_lambda_q1, b_lambda_k1,
           b_lambda_q2, b_lambda_k2, b_subln, w_out, g_final):
    depth = w_in.shape[0]
    hp, hs = x_prompt, x_sample
    for l in range(depth):
        lam_init = 0.8 - 0.6 * math.exp(-0.3 * l)
        args = (g_norm[l][None, :], w_in[l].T.astype(BF16), a_q_norm[l][:, None], a_k_norm[l][:, None],
                jnp.stack([b_lambda_q1[l], b_lambda_k1[l], b_lambda_q2[l], b_lambda_k2[l]]).astype(F32),
                b_subln[l][:, None], w_out[l].astype(BF16), g_final[None, :])
        final = l == depth - 1
        hp = _layer(hp, *args, lam_init, final)
        hs = _layer(hs, *args, lam_init, final)
    return (hp, hs)
```

```python
import functools
import math

import jax
import jax.numpy as jnp
from jax import lax
from jax.experimental import pallas as pl
from jax.experimental.pallas import tpu as pltpu

HEAD_DIM = 64
GRID_W = 64
ROPE_THETA = 10000.0
EPS = 1e-6
TOKEN_TILE = 512
SUB_KEYS = 256
HEADS_PER_STEP = 4
STEP_ROWS = HEADS_PER_STEP * HEAD_DIM
PHASE_PAIRS = 16
PROJ_TILES = 2
OUT_TILES = 4
BF16_SUBLANES = 16
LANES = 128
VMEM_LIMIT_BYTES = 56 * 1024 * 1024
LOG2E = math.log2(math.e)

F32 = jnp.float32
BF16 = jnp.bfloat16


def _dot_nt(a, b):
    return lax.dot_general(a, b, (((1,), (1,)), ((), ())), preferred_element_type=F32)


def _dot_tn(a, b):
    return lax.dot_general(a, b, (((0,), (0,)), ((), ())), preferred_element_type=F32)


def _dot(a, b):
    return jnp.dot(a, b, preferred_element_type=F32)


def _rope_axial(x, c, s):
    xs = jnp.concatenate([x[16:32], x[0:16], x[48:64], x[32:48]], axis=0)
    return x * c + xs * s


def _rope_1d(x, c, s):
    xs = jnp.concatenate([x[32:64], x[0:32]], axis=0)
    return x * c + xs * s


def _silu(g):
    return g * (1.0 / (1.0 + jnp.exp(-g)))


def _proj_kernel(x_ref, gn_ref, wt_ref, qn_ref, kn_ref, ca_ref, sa_ref, cb_ref, sb_ref,
                 qa_ref, va_ref, ga_ref, qb_ref, vb_ref, gb_ref, ka_ref, kb_ref, *, sizes):
    n_tiles, t = qa_ref.shape[1], qa_ref.shape[-1]
    for j in range(n_tiles):
        _proj_tile(j, slice(j * t, (j + 1) * t), x_ref, gn_ref, wt_ref, qn_ref, kn_ref, ca_ref, sa_ref, cb_ref,
                   sb_ref, qa_ref, va_ref, ga_ref, qb_ref, vb_ref, gb_ref, ka_ref, kb_ref, sizes)


def _proj_tile(j, tok, x_ref, gn_ref, wt_ref, qn_ref, kn_ref, ca_ref, sa_ref, cb_ref, sb_ref,
               qa_ref, va_ref, ga_ref, qb_ref, vb_ref, gb_ref, ka_ref, kb_ref, sizes):
    a_width, a_kv, b_qk, b_width = sizes
    x = x_ref[0, tok, :]
    t = x.shape[0]
    h = x * lax.rsqrt(jnp.mean(x * x, axis=-1, keepdims=True) + EPS) * gn_ref[...]
    hb = h.astype(BF16)

    def section(lo, n):
        return _dot_nt(wt_ref[lo:lo + n, :], hb)

    ca, sa, cb, sb = ca_ref[:, tok], sa_ref[:, tok], cb_ref[:, tok], sb_ref[:, tok]
    qscale = HEAD_DIM ** -0.5 * LOG2E
    ones_rows = (lax.broadcasted_iota(jnp.int32, (BF16_SUBLANES, t), 0) == 0).astype(BF16)

    def head_norm(blk, g):
        return blk * lax.rsqrt(jnp.mean(blk * blk, axis=0, keepdims=True) + EPS) * g

    def store_values(v_ref, vals, dv):
        dve = dv + BF16_SUBLANES
        for hd in range(vals.shape[0] // dv):
            v_ref[0, j, hd * dve:hd * dve + dv, :] = vals[hd * dv:(hd + 1) * dv].astype(BF16)
            v_ref[0, j, hd * dve + dv:(hd + 1) * dve, :] = ones_rows

    off = 0
    r = section(off, a_width)
    for hd in range(a_width // HEAD_DIM):
        blk = r[hd * HEAD_DIM:(hd + 1) * HEAD_DIM]
        y = _rope_axial(head_norm(blk, qn_ref[...]), ca, sa) * qscale
        qa_ref[0, j, hd * HEAD_DIM:(hd + 1) * HEAD_DIM, :] = y.astype(BF16)
    off += a_width

    r = section(off, 2 * a_kv)
    ks = []
    for hd in range(a_kv // HEAD_DIM):
        blk = r[hd * HEAD_DIM:(hd + 1) * HEAD_DIM]
        ks.append(_rope_axial(head_norm(blk, kn_ref[...]), ca, sa))
    ka_ref[0, tok, :] = jnp.concatenate(ks, axis=0).T.astype(BF16)
    store_values(va_ref, r[a_kv:2 * a_kv], HEAD_DIM)
    off += 2 * a_kv

    ga_ref[0, j] = _silu(section(off, a_width)).astype(BF16)
    off += a_width

    r = section(off, b_qk)
    for hd in range(b_qk // HEAD_DIM):
        blk = r[hd * HEAD_DIM:(hd + 1) * HEAD_DIM]
        qb_ref[0, j, hd * HEAD_DIM:(hd + 1) * HEAD_DIM, :] = (_rope_1d(blk, cb, sb) * qscale).astype(BF16)
    off += b_qk

    r = section(off, b_qk)
    ks = [_rope_1d(r[hd * HEAD_DIM:(hd + 1) * HEAD_DIM], cb, sb) for hd in range(b_qk // HEAD_DIM)]
    kb_ref[0, tok, :] = jnp.concatenate(ks, axis=0).T.astype(BF16)
    off += b_qk

    store_values(vb_ref, section(off, b_width), 2 * HEAD_DIM)
    off += b_width
    gb_ref[0, j] = _silu(section(off, b_width)).astype(BF16)


def _attn_kernel(*refs, mode, n_chunks, lam_init, grid_dims, group, units):
    n_s = 2 * group
    if mode == "A":
        q_ref, qn_ref, k_ref, kn_ref, v_ref, g_ref, o_ref = refs[:7]
        dv = HEAD_DIM
    else:
        q_ref, qn_ref, k_ref, kn_ref, v_ref, g_ref, lam_ref, sub_ref, o_ref = refs[:9]
        dv = 2 * HEAD_DIM
    s_refs, (acc_ref, mcar_ref) = refs[-2 - n_s:-2], refs[-2:]
    dve = dv + BF16_SUBLANES
    nq, tq, tk = q_ref.shape[1], q_ref.shape[-1], v_ref.shape[-1]
    n_maps = units * nq * HEADS_PER_STEP
    n_phases = n_maps // group
    assert n_maps % group == 0 and n_phases % 2 == 0
    slots = k_ref.shape[-1] // HEAD_DIM
    n_sub = tk // SUB_KEYS
    _, nu, ni = grid_dims
    b, u, i = pl.program_id(0), pl.program_id(1), pl.program_id(2)
    is_first = (b == 0) & (u == 0) & (i == 0)
    u_next = (u + (i + 1) // ni) % nu
    z = jnp.zeros((HEAD_DIM, tq), BF16)

    def split(m):
        ul, rest = divmod(m, nq * HEADS_PER_STEP)
        tl, hd = divmod(rest, HEADS_PER_STEP)
        return ul, tl, hd

    def place(rows64, slot):
        rows = [z] * slots
        rows[slot] = rows64
        return jnp.concatenate(rows, axis=0)

    def expand(rows64, hd, step_unit, ul):
        if mode == "B":
            return place(rows64, hd)
        if nu == 1:
            return place(rows64, ul)
        return jnp.where(step_unit * units + ul == 0, place(rows64, 0), place(rows64, 1))

    def q_rows(ul, hd):
        return slice(ul * STEP_ROWS + hd * HEAD_DIM, ul * STEP_ROWS + (hd + 1) * HEAD_DIM)

    qx = []
    for m in range(n_maps):
        ul, tl, hd = split(m)
        qx.append(expand(q_ref[0, tl, q_rows(ul, hd), :], hd, u, ul))
    qx_next = [expand(qn_ref[0, 0, q_rows(0, m), :], m, u_next, 0) for m in range(group)]

    def v_rows(m):
        ul, _, hd = split(m)
        blk = ul if mode == "A" else ul * (HEADS_PER_STEP // 2) + hd // 2
        return slice(blk * dve, (blk + 1) * dve)

    def keys(c, j):
        return pl.ds(c * tk + j * SUB_KEYS, SUB_KEYS)

    def pass1(kref, qxm, s_dst, c, j, m8):
        s = _dot(kref[0, keys(c, j), :], qxm)
        s_dst[keys(c, j), :] = s
        return jnp.maximum(m8, jnp.max(s.reshape(SUB_KEYS // 8, 8, tq), axis=0))

    def pass2(m, s_src, mb, c, j):
        s = s_src[keys(c, j), :]
        p = jnp.exp2(s.reshape(SUB_KEYS // 8, 8, tq) - mb[None]).reshape(SUB_KEYS, tq).astype(BF16)
        vs = v_ref[0, c, v_rows(m), j * SUB_KEYS:(j + 1) * SUB_KEYS]
        return _dot(vs, p)

    neg = jnp.full((8, tq), -jnp.inf, F32)

    def run_phase(trips, p1s, p2s):
        def phase(_, m8s):
            m8s = list(m8s)
            for c in range(n_chunks):
                parts = [None] * len(p2s)
                for j in range(n_sub):
                    for k in range(max(len(p1s), len(p2s))):
                        if k < len(p1s):
                            m8s[k] = pass1(*p1s[k], c, j, m8s[k])
                        if k < len(p2s):
                            d = pass2(*p2s[k], c, j)
                            parts[k] = d if parts[k] is None else parts[k] + d
                for k, p2 in enumerate(p2s):
                    if c == 0:
                        acc_ref[p2[0]] = parts[k]
                    else:
                        acc_ref[p2[0]] += parts[k]
            return tuple(m8s)

        return lax.fori_loop(0, trips, phase, (neg,) * len(p1s))

    def col_max(m8):
        return jnp.broadcast_to(jnp.max(m8, axis=0, keepdims=True), (8, tq))

    def s_buf(parity, k):
        return s_refs[parity * group + k]

    m_first = run_phase(is_first.astype(jnp.int32), [(k_ref, qx[k], s_buf(0, k)) for k in range(group)], [])

    @pl.when(is_first)
    def _():
        for k in range(group):
            mcar_ref[k] = m_first[k]

    once = jnp.minimum(i, 0) + 1
    m8s = [mcar_ref[k] for k in range(group)]
    for ph in range(1, n_phases + 1):
        if ph < n_phases:
            p1s = [(k_ref, qx[ph * group + k], s_buf(ph % 2, k)) for k in range(group)]
        else:
            p1s = [(kn_ref, qx_next[k], s_buf(0, k)) for k in range(group)]
        p2s = [((ph - 1) * group + k, s_buf((ph - 1) % 2, k), col_max(m8s[k])) for k in range(group)]
        m8s = run_phase(once, p1s, p2s)
    for k in range(group):
        mcar_ref[k] = m8s[k]

    def normalised(m, lanes):
        return acc_ref[m, 0:dv, lanes] * (1.0 / acc_ref[m, dv:dv + 1, lanes])

    if mode == "B":
        lv = lam_ref[...]
        lam = (jnp.exp(jnp.sum(lv[0:1] * lv[1:2], axis=-1, keepdims=True))
               - jnp.exp(jnp.sum(lv[2:3] * lv[3:4], axis=-1, keepdims=True)) + lam_init)
        sub_gain = sub_ref[...] * (1.0 - lam_init)
    for ul in range(units):
        for t_idx in range(nq):
            base = (ul * nq + t_idx) * HEADS_PER_STEP
            for strip in range(tq // LANES):
                lanes = slice(strip * LANES, (strip + 1) * LANES)
                for j in range(STEP_ROWS // dv):
                    rows = slice(ul * STEP_ROWS + j * dv, ul * STEP_ROWS + (j + 1) * dv)
                    g = g_ref[0, t_idx, rows, lanes].astype(F32)
                    if mode == "A":
                        y = normalised(base + j, lanes)
                    else:
                        o = normalised(base + 2 * j, lanes) - lam * normalised(base + 2 * j + 1, lanes)
                        y = o * lax.rsqrt(jnp.mean(o * o, axis=0, keepdims=True) + EPS) * sub_gain
                    o_ref[0, t_idx, rows, lanes] = (y * g).astype(BF16)


def _out_kernel(x_ref, oa_ref, ob_ref, w_ref, gf_ref, y_ref, *, a_width, final):
    n_tiles, t = oa_ref.shape[1], oa_ref.shape[-1]
    for j in range(n_tiles):
        tok = slice(j * t, (j + 1) * t)
        y = (x_ref[0, tok, :] + _dot_tn(oa_ref[0, j], w_ref[0:a_width, :])
             + _dot_tn(ob_ref[0, j], w_ref[a_width:, :]))
        if final:
            y = y * lax.rsqrt(jnp.mean(y * y, axis=-1, keepdims=True) + EPS) * gf_ref[...]
        y_ref[0, tok, :] = y


def _rope_tables(seq):
    def cs(pos, dim):
        inv = ROPE_THETA ** (-jnp.arange(0, dim, 2, dtype=F32) / dim)
        ang = pos[None, :] * inv[:, None]
        return jnp.cos(ang), jnp.sin(ang)

    rows = seq // GRID_W
    row = jnp.repeat(jnp.arange(rows, dtype=F32), GRID_W)
    col = jnp.tile(jnp.arange(GRID_W, dtype=F32), rows)
    pos = jnp.arange(seq, dtype=F32)
    cr, sr = cs(row, HEAD_DIM // 2)
    cc, sc = cs(col, HEAD_DIM // 2)
    cp, sp = cs(pos, HEAD_DIM)
    ca = jnp.concatenate([cr, cr, cc, cc], axis=0)
    sa = jnp.concatenate([-sr, sr, -sc, sc], axis=0)
    cb = jnp.concatenate([cp, cp], axis=0)
    sb = jnp.concatenate([-sp, sp], axis=0)
    return ca, sa, cb, sb


def _layer(x, gn, wt, qn, kn, lam_vecs, subln, w_out, g_final, lam_init, final):
    bsz, seq, d = x.shape
    a_width = (d // (2 * HEAD_DIM)) * HEAD_DIM
    a_kv = max(1, (a_width // HEAD_DIM) // 4) * HEAD_DIM
    b_qk = (d // (4 * HEAD_DIM)) * 2 * HEAD_DIM
    b_width = b_qk
    assert a_width == HEADS_PER_STEP * a_kv and a_width % STEP_ROWS == 0 and b_width % STEP_ROWS == 0
    t = TOKEN_TILE
    assert seq % t == 0 and seq % GRID_W == 0
    nc = seq // t
    va_rows = (a_kv // HEAD_DIM) * (HEAD_DIM + BF16_SUBLANES)
    vb_rows = (b_width // (2 * HEAD_DIM)) * (2 * HEAD_DIM + BF16_SUBLANES)
    ca, sa, cb, sb = _rope_tables(seq)
    cparams = functools.partial(pltpu.CompilerParams, vmem_limit_bytes=VMEM_LIMIT_BYTES)

    def fm(rows):
        return jax.ShapeDtypeStruct((bsz, nc, rows, t), BF16)

    pt = PROJ_TILES
    assert nc % pt == 0
    fm_spec = lambda rows: pl.BlockSpec((1, pt, rows, t), lambda b, i: (b, i, 0, 0))
    tab_spec = pl.BlockSpec((HEAD_DIM, pt * t), lambda b, i: (0, i))
    full = lambda shape: pl.BlockSpec(shape, lambda b, i: (0,) * len(shape))

    qa, va, ga, qb, vb, gb, ka, kb = pl.pallas_call(
        functools.partial(_proj_kernel, sizes=(a_width, a_kv, b_qk, b_width)),
        grid=(bsz, nc // pt),
        in_specs=[pl.BlockSpec((1, pt * t, d), lambda b, i: (b, i, 0)), full((1, d)), full(wt.shape),
                  full((HEAD_DIM, 1)), full((HEAD_DIM, 1)), tab_spec, tab_spec, tab_spec, tab_spec],
        out_specs=[fm_spec(a_width), fm_spec(va_rows), fm_spec(a_width), fm_spec(b_qk), fm_spec(vb_rows),
                   fm_spec(b_width),
                   pl.BlockSpec((1, pt * t, a_kv), lambda b, i: (b, i, 0)),
                   pl.BlockSpec((1, pt * t, b_qk), lambda b, i: (b, i, 0))],
        out_shape=[fm(a_width), fm(va_rows), fm(a_width), fm(b_qk), fm(vb_rows), fm(b_width),
                   jax.ShapeDtypeStruct((bsz, seq, a_kv), BF16),
                   jax.ShapeDtypeStruct((bsz, seq, b_qk), BF16)],
        compiler_params=cparams(dimension_semantics=("parallel", "parallel")),
        name="proj",
    )(x, gn, wt, qn, kn, ca, sa, cb, sb)

    group = max(1, min(HEADS_PER_STEP, PHASE_PAIRS // nc))
    nq = max(1, 2 * group // HEADS_PER_STEP)
    assert nc % nq == 0
    units_a = a_width // STEP_ROWS
    units_b = 1
    grid_a = (bsz, a_width // STEP_ROWS // units_a, nc // nq)
    grid_b = (bsz, b_width // STEP_ROWS // units_b, nc // nq)

    def next_step(dims):
        nb, nu, ni = dims

        def f(b, u, i):
            cu = (i + 1) // ni
            cb = (u + cu) // nu
            return jnp.minimum(b + cb, nb - 1), (u + cu) % nu, (i + 1) % ni
        return f

    step_spec = lambda units: pl.BlockSpec((1, nq, units * STEP_ROWS, t), lambda b, u, i: (b, i, u, 0))
    sem = ("arbitrary", "arbitrary", "arbitrary")

    def next_q_spec(dims, units):
        nxt = next_step(dims)
        rows = group * HEAD_DIM

        def index(b, u, i):
            b2, u2, i2 = nxt(b, u, i)
            return b2, i2 * nq, u2 * (units * STEP_ROWS // rows), 0
        return pl.BlockSpec((1, 1, rows, t), index)

    def scratch(dve, units):
        return ([pltpu.VMEM((seq, t), F32)] * (2 * group)
                + [pltpu.VMEM((units * nq * HEADS_PER_STEP, dve, t), F32), pltpu.VMEM((group, 8, t), F32)])

    dve_a = HEAD_DIM + BF16_SUBLANES
    ka_spec = lambda index: pl.BlockSpec((1, seq, a_kv), index)
    oa = pl.pallas_call(
        functools.partial(_attn_kernel, mode="A", n_chunks=nc, lam_init=lam_init, grid_dims=grid_a, group=group,
                          units=units_a),
        grid=grid_a,
        in_specs=[step_spec(units_a), next_q_spec(grid_a, units_a),
                  ka_spec(lambda b, u, i: (b, 0, 0)),
                  ka_spec(lambda b, u, i: (next_step(grid_a)(b, u, i)[0], 0, 0)),
                  pl.BlockSpec((1, nc, units_a * dve_a, t), lambda b, u, i: (b, 0, u, 0)),
                  step_spec(units_a)],
        out_specs=step_spec(units_a),
        out_shape=fm(a_width),
        scratch_shapes=scratch(dve_a, units_a),
        compiler_params=cparams(dimension_semantics=sem),
        name="attn_a",
    )(qa, qa, ka, ka, va, ga)

    dve_b = 2 * HEAD_DIM + BF16_SUBLANES
    kb_spec = lambda index: pl.BlockSpec((1, seq, STEP_ROWS), index)

    def kb_next(b, u, i):
        b2, u2, _ = next_step(grid_b)(b, u, i)
        return b2, 0, u2

    ob = pl.pallas_call(
        functools.partial(_attn_kernel, mode="B", n_chunks=nc, lam_init=lam_init, grid_dims=grid_b, group=group,
                          units=units_b),
        grid=grid_b,
        in_specs=[step_spec(units_b), next_q_spec(grid_b, units_b),
                  kb_spec(lambda b, u, i: (b, 0, u)),
                  kb_spec(kb_next),
                  pl.BlockSpec((1, nc, 2 * dve_b, t), lambda b, u, i: (b, 0, u, 0)),
                  step_spec(units_b),
                  pl.BlockSpec((4, HEAD_DIM), lambda b, u, i: (0, 0)),
                  pl.BlockSpec((2 * HEAD_DIM, 1), lambda b, u, i: (0, 0))],
        out_specs=step_spec(units_b),
        out_shape=fm(b_width),
        scratch_shapes=scratch(dve_b, units_b),
        compiler_params=cparams(dimension_semantics=sem),
        name="attn_b",
    )(qb, qb, kb, kb, vb, gb, lam_vecs, subln)

    ot = OUT_TILES
    assert nc % ot == 0
    out_fm_spec = lambda rows: pl.BlockSpec((1, ot, rows, t), lambda b, i: (b, i, 0, 0))
    return pl.pallas_call(
        functools.partial(_out_kernel, a_width=a_width, final=final),
        grid=(bsz, nc // ot),
        in_specs=[pl.BlockSpec((1, ot * t, d), lambda b, i: (b, i, 0)), out_fm_spec(a_width), out_fm_spec(b_width),
                  full(w_out.shape), full((1, d))],
        out_specs=pl.BlockSpec((1, ot * t, d), lambda b, i: (b, i, 0)),
        out_shape=jax.ShapeDtypeStruct(x.shape, x.dtype),
        compiler_params=cparams(dimension_semantics=("parallel", "parallel")),
        name="out_proj",
    )(x, oa, ob, w_out, g_final)


def kernel(x_prompt, x_sample, g_norm, w_in, a_q_norm, a_k_norm, b_lambda_q1, b_lambda_k1,
           b_lambda_q2, b_lambda_k2, b_subln, w_out, g_final):
    depth = w_in.shape[0]
    hp, hs = x_prompt, x_sample
    for l in range(depth):
        lam_init = 0.8 - 0.6 * math.exp(-0.3 * l)
        args = (g_norm[l][None, :], w_in[l].T.astype(BF16), a_q_norm[l][:, None], a_k_norm[l][:, None],
                jnp.stack([b_lambda_q1[l], b_lambda_k1[l], b_lambda_q2[l], b_lambda_k2[l]]).astype(F32),
                b_subln[l][:, None], w_out[l].astype(BF16), g_final[None, :])
        final = l == depth - 1
        hp = _layer(hp, *args, lam_init, final)
        hs = _layer(hs, *args, lam_init, final)
    return (hp, hs)
```

```python
import functools
import math

import jax
import jax.numpy as jnp
from jax import lax
from jax.experimental import pallas as pl
from jax.experimental.pallas import tpu as pltpu

HEAD_DIM = 64
GRID_W = 64
ROPE_THETA = 10000.0
EPS = 1e-6
TOKEN_TILE = 512
SUB_KEYS = 256
HEADS_PER_STEP = 4
STEP_ROWS = HEADS_PER_STEP * HEAD_DIM
PHASE_PAIRS = 16
PROJ_TILES = 2
OUT_TILES = 4
BF16_SUBLANES = 16
LANES = 128
KEY_LANES = 256
VMEM_LIMIT_BYTES = 56 * 1024 * 1024
LOG2E = math.log2(math.e)

F32 = jnp.float32
BF16 = jnp.bfloat16


def _dot_nt(a, b):
    return lax.dot_general(a, b, (((1,), (1,)), ((), ())), preferred_element_type=F32)


def _dot_tn(a, b):
    return lax.dot_general(a, b, (((0,), (0,)), ((), ())), preferred_element_type=F32)


def _dot(a, b):
    return jnp.dot(a, b, preferred_element_type=F32)


def _rope_axial(x, c, s):
    xs = jnp.concatenate([x[16:32], x[0:16], x[48:64], x[32:48]], axis=0)
    return x * c + xs * s


def _rope_1d(x, c, s):
    xs = jnp.concatenate([x[32:64], x[0:32]], axis=0)
    return x * c + xs * s


def _silu(g):
    return g * (1.0 / (1.0 + jnp.exp(-g)))


def _proj_kernel(x_ref, gn_ref, wt_ref, qn_ref, kn_ref, ca_ref, sa_ref, cb_ref, sb_ref,
                 qa_ref, va_ref, ga_ref, qb_ref, vb_ref, gb_ref, ka_ref, kb_ref, *, sizes):
    n_tiles, t = qa_ref.shape[1], qa_ref.shape[-1]
    for j in range(n_tiles):
        _proj_tile(j, slice(j * t, (j + 1) * t), x_ref, gn_ref, wt_ref, qn_ref, kn_ref, ca_ref, sa_ref, cb_ref,
                   sb_ref, qa_ref, va_ref, ga_ref, qb_ref, vb_ref, gb_ref, ka_ref, kb_ref, sizes)


def _proj_tile(j, tok, x_ref, gn_ref, wt_ref, qn_ref, kn_ref, ca_ref, sa_ref, cb_ref, sb_ref,
               qa_ref, va_ref, ga_ref, qb_ref, vb_ref, gb_ref, ka_ref, kb_ref, sizes):
    a_width, a_kv, b_qk, b_width = sizes
    x = x_ref[0, tok, :]
    t = x.shape[0]
    h = x * lax.rsqrt(jnp.mean(x * x, axis=-1, keepdims=True) + EPS) * gn_ref[...]
    hb = h.astype(BF16)

    def section(lo, n):
        return _dot_nt(wt_ref[lo:lo + n, :], hb)

    ca, sa, cb, sb = ca_ref[:, tok], sa_ref[:, tok], cb_ref[:, tok], sb_ref[:, tok]
    qscale = HEAD_DIM ** -0.5 * LOG2E
    ones_rows = (lax.broadcasted_iota(jnp.int32, (BF16_SUBLANES, t), 0) == 0).astype(BF16)

    def head_norm(blk, g):
        return blk * lax.rsqrt(jnp.mean(blk * blk, axis=0, keepdims=True) + EPS) * g

    def store_values(v_ref, vals, dv):
        dve = dv + BF16_SUBLANES
        for hd in range(vals.shape[0] // dv):
            v_ref[0, j, hd * dve:hd * dve + dv, :] = vals[hd * dv:(hd + 1) * dv].astype(BF16)
            v_ref[0, j, hd * dve + dv:(hd + 1) * dve, :] = ones_rows

    off = 0
    r = section(off, a_width)
    for hd in range(a_width // HEAD_DIM):
        blk = r[hd * HEAD_DIM:(hd + 1) * HEAD_DIM]
        y = _rope_axial(head_norm(blk, qn_ref[...]), ca, sa) * qscale
        qa_ref[0, j, hd * HEAD_DIM:(hd + 1) * HEAD_DIM, :] = y.astype(BF16)
    off += a_width

    r = section(off, 2 * a_kv)
    ks = []
    for hd in range(a_kv // HEAD_DIM):
        blk = r[hd * HEAD_DIM:(hd + 1) * HEAD_DIM]
        ks.append(_rope_axial(head_norm(blk, kn_ref[...]), ca, sa))
    ka_ref[0, tok, 0:a_kv] = jnp.concatenate(ks, axis=0).T.astype(BF16)
    ka_ref[0, tok, a_kv:] = jnp.zeros((t, ka_ref.shape[-1] - a_kv), BF16)
    store_values(va_ref, r[a_kv:2 * a_kv], HEAD_DIM)
    off += 2 * a_kv

    ga_ref[0, j] = _silu(section(off, a_width)).astype(BF16)
    off += a_width

    r = section(off, b_qk)
    for hd in range(b_qk // HEAD_DIM):
        blk = r[hd * HEAD_DIM:(hd + 1) * HEAD_DIM]
        qb_ref[0, j, hd * HEAD_DIM:(hd + 1) * HEAD_DIM, :] = (_rope_1d(blk, cb, sb) * qscale).astype(BF16)
    off += b_qk

    r = section(off, b_qk)
    ks = [_rope_1d(r[hd * HEAD_DIM:(hd + 1) * HEAD_DIM], cb, sb) for hd in range(b_qk // HEAD_DIM)]
    kb_ref[0, tok, :] = jnp.concatenate(ks, axis=0).T.astype(BF16)
    off += b_qk

    store_values(vb_ref, section(off, b_width), 2 * HEAD_DIM)
    off += b_width
    gb_ref[0, j] = _silu(section(off, b_width)).astype(BF16)


def _attn_kernel(*refs, mode, n_chunks, lam_init, grid_dims, group, units):
    n_s = 2 * group
    if mode == "A":
        q_ref, qn_ref, k_ref, kn_ref, v_ref, g_ref, o_ref = refs[:7]
        dv = HEAD_DIM
    else:
        q_ref, qn_ref, k_ref, kn_ref, v_ref, g_ref, lam_ref, sub_ref, o_ref = refs[:9]
        dv = 2 * HEAD_DIM
    s_refs, (acc_ref, mcar_ref) = refs[-2 - n_s:-2], refs[-2:]
    dve = dv + BF16_SUBLANES
    nq, tq, tk = q_ref.shape[1], q_ref.shape[-1], v_ref.shape[-1]
    n_maps = units * nq * HEADS_PER_STEP
    n_phases = n_maps // group
    assert n_maps % group == 0 and n_phases % 2 == 0
    slots = k_ref.shape[-1] // HEAD_DIM
    n_sub = tk // SUB_KEYS
    _, nu, ni = grid_dims
    b, u, i = pl.program_id(0), pl.program_id(1), pl.program_id(2)
    is_first = (b == 0) & (u == 0) & (i == 0)
    u_next = (u + (i + 1) // ni) % nu
    z = jnp.zeros((HEAD_DIM, tq), BF16)

    def split(m):
        ul, rest = divmod(m, nq * HEADS_PER_STEP)
        tl, hd = divmod(rest, HEADS_PER_STEP)
        return ul, tl, hd

    def place(rows64, slot):
        rows = [z] * slots
        rows[slot] = rows64
        return jnp.concatenate(rows, axis=0)

    def expand(rows64, hd, step_unit, ul):
        if mode == "B":
            return place(rows64, hd)
        if nu == 1:
            return place(rows64, ul)
        return jnp.where(step_unit * units + ul == 0, place(rows64, 0), place(rows64, 1))

    def q_rows(ul, hd):
        return slice(ul * STEP_ROWS + hd * HEAD_DIM, ul * STEP_ROWS + (hd + 1) * HEAD_DIM)

    qx = []
    for m in range(n_maps):
        ul, tl, hd = split(m)
        qx.append(expand(q_ref[0, tl, q_rows(ul, hd), :], hd, u, ul))
    qx_next = [expand(qn_ref[0, 0, q_rows(0, m), :], m, u_next, 0) for m in range(group)]

    def v_rows(m):
        ul, _, hd = split(m)
        blk = ul if mode == "A" else ul * (HEADS_PER_STEP // 2) + hd // 2
        return slice(blk * dve, (blk + 1) * dve)

    def keys(c, j):
        return pl.ds(c * tk + j * SUB_KEYS, SUB_KEYS)

    def pass1(kref, qxm, s_dst, c, j, m8):
        s = _dot(kref[0, keys(c, j), :], qxm)
        s_dst[keys(c, j), :] = s
        return jnp.maximum(m8, jnp.max(s.reshape(SUB_KEYS // 8, 8, tq), axis=0))

    def pass2(m, s_src, mb, c, j):
        s = s_src[keys(c, j), :]
        p = jnp.exp2(s.reshape(SUB_KEYS // 8, 8, tq) - mb[None]).reshape(SUB_KEYS, tq).astype(BF16)
        vs = v_ref[0, c, v_rows(m), j * SUB_KEYS:(j + 1) * SUB_KEYS]
        return _dot(vs, p)

    neg = jnp.full((8, tq), -jnp.inf, F32)

    def run_phase(trips, p1s, p2s):
        def phase(_, m8s):
            m8s = list(m8s)
            for c in range(n_chunks):
                parts = [None] * len(p2s)
                for j in range(n_sub):
                    for k in range(max(len(p1s), len(p2s))):
                        if k < len(p1s):
                            m8s[k] = pass1(*p1s[k], c, j, m8s[k])
                        if k < len(p2s):
                            d = pass2(*p2s[k], c, j)
                            parts[k] = d if parts[k] is None else parts[k] + d
                for k, p2 in enumerate(p2s):
                    if c == 0:
                        acc_ref[p2[0]] = parts[k]
                    else:
                        acc_ref[p2[0]] += parts[k]
            return tuple(m8s)

        return lax.fori_loop(0, trips, phase, (neg,) * len(p1s))

    def col_max(m8):
        return jnp.broadcast_to(jnp.max(m8, axis=0, keepdims=True), (8, tq))

    def s_buf(parity, k):
        return s_refs[parity * group + k]

    m_first = run_phase(is_first.astype(jnp.int32), [(k_ref, qx[k], s_buf(0, k)) for k in range(group)], [])

    @pl.when(is_first)
    def _():
        for k in range(group):
            mcar_ref[k] = m_first[k]

    once = jnp.minimum(i, 0) + 1
    m8s = [mcar_ref[k] for k in range(group)]
    for ph in range(1, n_phases + 1):
        if ph < n_phases:
            p1s = [(k_ref, qx[ph * group + k], s_buf(ph % 2, k)) for k in range(group)]
        else:
            p1s = [(kn_ref, qx_next[k], s_buf(0, k)) for k in range(group)]
        p2s = [((ph - 1) * group + k, s_buf((ph - 1) % 2, k), col_max(m8s[k])) for k in range(group)]
        m8s = run_phase(once, p1s, p2s)
    for k in range(group):
        mcar_ref[k] = m8s[k]

    def normalised(m, lanes):
        return acc_ref[m, 0:dv, lanes] * (1.0 / acc_ref[m, dv:dv + 1, lanes])

    if mode == "B":
        lv = lam_ref[...]
        lam = (jnp.exp(jnp.sum(lv[0:1] * lv[1:2], axis=-1, keepdims=True))
               - jnp.exp(jnp.sum(lv[2:3] * lv[3:4], axis=-1, keepdims=True)) + lam_init)
        sub_gain = sub_ref[...] * (1.0 - lam_init)
    for ul in range(units):
        for t_idx in range(nq):
            base = (ul * nq + t_idx) * HEADS_PER_STEP
            for strip in range(tq // LANES):
                lanes = slice(strip * LANES, (strip + 1) * LANES)
                for j in range(STEP_ROWS // dv):
                    rows = slice(ul * STEP_ROWS + j * dv, ul * STEP_ROWS + (j + 1) * dv)
                    g = g_ref[0, t_idx, rows, lanes].astype(F32)
                    if mode == "A":
                        y = normalised(base + j, lanes)
                    else:
                        o = normalised(base + 2 * j, lanes) - lam * normalised(base + 2 * j + 1, lanes)
                        y = o * lax.rsqrt(jnp.mean(o * o, axis=0, keepdims=True) + EPS) * sub_gain
                    o_ref[0, t_idx, rows, lanes] = (y * g).astype(BF16)


def _out_kernel(x_ref, oa_ref, ob_ref, w_ref, gf_ref, y_ref, *, a_width, final):
    n_tiles, t = oa_ref.shape[1], oa_ref.shape[-1]
    for j in range(n_tiles):
        tok = slice(j * t, (j + 1) * t)
        y = (x_ref[0, tok, :] + _dot_tn(oa_ref[0, j], w_ref[0:a_width, :])
             + _dot_tn(ob_ref[0, j], w_ref[a_width:, :]))
        if final:
            y = y * lax.rsqrt(jnp.mean(y * y, axis=-1, keepdims=True) + EPS) * gf_ref[...]
        y_ref[0, tok, :] = y


def _rope_tables(seq):
    def cs(pos, dim):
        inv = ROPE_THETA ** (-jnp.arange(0, dim, 2, dtype=F32) / dim)
        ang = pos[None, :] * inv[:, None]
        return jnp.cos(ang), jnp.sin(ang)

    rows = seq // GRID_W
    row = jnp.repeat(jnp.arange(rows, dtype=F32), GRID_W)
    col = jnp.tile(jnp.arange(GRID_W, dtype=F32), rows)
    pos = jnp.arange(seq, dtype=F32)
    cr, sr = cs(row, HEAD_DIM // 2)
    cc, sc = cs(col, HEAD_DIM // 2)
    cp, sp = cs(pos, HEAD_DIM)
    ca = jnp.concatenate([cr, cr, cc, cc], axis=0)
    sa = jnp.concatenate([-sr, sr, -sc, sc], axis=0)
    cb = jnp.concatenate([cp, cp], axis=0)
    sb = jnp.concatenate([-sp, sp], axis=0)
    return ca, sa, cb, sb


def _layer(x, gn, wt, qn, kn, lam_vecs, subln, w_out, g_final, lam_init, final):
    bsz, seq, d = x.shape
    a_width = (d // (2 * HEAD_DIM)) * HEAD_DIM
    a_kv = max(1, (a_width // HEAD_DIM) // 4) * HEAD_DIM
    b_qk = (d // (4 * HEAD_DIM)) * 2 * HEAD_DIM
    b_width = b_qk
    assert a_width == HEADS_PER_STEP * a_kv and a_width % STEP_ROWS == 0 and b_width % STEP_ROWS == 0
    t = TOKEN_TILE
    assert seq % t == 0 and seq % GRID_W == 0
    nc = seq // t
    va_rows = (a_kv // HEAD_DIM) * (HEAD_DIM + BF16_SUBLANES)
    vb_rows = (b_width // (2 * HEAD_DIM)) * (2 * HEAD_DIM + BF16_SUBLANES)
    ca, sa, cb, sb = _rope_tables(seq)
    cparams = functools.partial(pltpu.CompilerParams, vmem_limit_bytes=VMEM_LIMIT_BYTES)

    def fm(rows):
        return jax.ShapeDtypeStruct((bsz, nc, rows, t), BF16)

    pt = PROJ_TILES
    assert nc % pt == 0
    fm_spec = lambda rows: pl.BlockSpec((1, pt, rows, t), lambda b, i: (b, i, 0, 0))
    tab_spec = pl.BlockSpec((HEAD_DIM, pt * t), lambda b, i: (0, i))
    full = lambda shape: pl.BlockSpec(shape, lambda b, i: (0,) * len(shape))

    qa, va, ga, qb, vb, gb, ka, kb = pl.pallas_call(
        functools.partial(_proj_kernel, sizes=(a_width, a_kv, b_qk, b_width)),
        grid=(bsz, nc // pt),
        in_specs=[pl.BlockSpec((1, pt * t, d), lambda b, i: (b, i, 0)), full((1, d)), full(wt.shape),
                  full((HEAD_DIM, 1)), full((HEAD_DIM, 1)), tab_spec, tab_spec, tab_spec, tab_spec],
        out_specs=[fm_spec(a_width), fm_spec(va_rows), fm_spec(a_width), fm_spec(b_qk), fm_spec(vb_rows),
                   fm_spec(b_width),
                   pl.BlockSpec((1, pt * t, KEY_LANES), lambda b, i: (b, i, 0)),
                   pl.BlockSpec((1, pt * t, b_qk), lambda b, i: (b, i, 0))],
        out_shape=[fm(a_width), fm(va_rows), fm(a_width), fm(b_qk), fm(vb_rows), fm(b_width),
                   jax.ShapeDtypeStruct((bsz, seq, KEY_LANES), BF16),
                   jax.ShapeDtypeStruct((bsz, seq, b_qk), BF16)],
        compiler_params=cparams(dimension_semantics=("parallel", "parallel")),
        name="proj",
    )(x, gn, wt, qn, kn, ca, sa, cb, sb)

    group = max(1, min(HEADS_PER_STEP, PHASE_PAIRS // nc))
    nq = max(1, 2 * group // HEADS_PER_STEP)
    assert nc % nq == 0
    units_a = a_width // STEP_ROWS
    units_b = 1
    grid_a = (bsz, a_width // STEP_ROWS // units_a, nc // nq)
    grid_b = (bsz, b_width // STEP_ROWS // units_b, nc // nq)

    def next_step(dims):
        nb, nu, ni = dims

        def f(b, u, i):
            cu = (i + 1) // ni
            cb = (u + cu) // nu
            return jnp.minimum(b + cb, nb - 1), (u + cu) % nu, (i + 1) % ni
        return f

    step_spec = lambda units: pl.BlockSpec((1, nq, units * STEP_ROWS, t), lambda b, u, i: (b, i, u, 0))
    sem = ("arbitrary", "arbitrary", "arbitrary")

    def next_q_spec(dims, units):
        nxt = next_step(dims)
        rows = group * HEAD_DIM

        def index(b, u, i):
            b2, u2, i2 = nxt(b, u, i)
            return b2, i2 * nq, u2 * (units * STEP_ROWS // rows), 0
        return pl.BlockSpec((1, 1, rows, t), index)

    def scratch(dve, units):
        return ([pltpu.VMEM((seq, t), F32)] * (2 * group)
                + [pltpu.VMEM((units * nq * HEADS_PER_STEP, dve, t), F32), pltpu.VMEM((group, 8, t), F32)])

    dve_a = HEAD_DIM + BF16_SUBLANES
    ka_spec = lambda index: pl.BlockSpec((1, seq, KEY_LANES), index)
    oa = pl.pallas_call(
        functools.partial(_attn_kernel, mode="A", n_chunks=nc, lam_init=lam_init, grid_dims=grid_a, group=group,
                          units=units_a),
        grid=grid_a,
        in_specs=[step_spec(units_a), next_q_spec(grid_a, units_a),
                  ka_spec(lambda b, u, i: (b, 0, 0)),
                  ka_spec(lambda b, u, i: (next_step(grid_a)(b, u, i)[0], 0, 0)),
                  pl.BlockSpec((1, nc, units_a * dve_a, t), lambda b, u, i: (b, 0, u, 0)),
                  step_spec(units_a)],
        out_specs=step_spec(units_a),
        out_shape=fm(a_width),
        scratch_shapes=scratch(dve_a, units_a),
        compiler_params=cparams(dimension_semantics=sem),
        name="attn_a",
    )(qa, qa, ka, ka, va, ga)

    dve_b = 2 * HEAD_DIM + BF16_SUBLANES
    kb_spec = lambda index: pl.BlockSpec((1, seq, STEP_ROWS), index)

    def kb_next(b, u, i):
        b2, u2, _ = next_step(grid_b)(b, u, i)
        return b2, 0, u2

    ob = pl.pallas_call(
        functools.partial(_attn_kernel, mode="B", n_chunks=nc, lam_init=lam_init, grid_dims=grid_b, group=group,
                          units=units_b),
        grid=grid_b,
        in_specs=[step_spec(units_b), next_q_spec(grid_b, units_b),
                  kb_spec(lambda b, u, i: (b, 0, u)),
                  kb_spec(kb_next),
                  pl.BlockSpec((1, nc, 2 * dve_b, t), lambda b, u, i: (b, 0, u, 0)),
                  step_spec(units_b),
                  pl.BlockSpec((4, HEAD_DIM), lambda b, u, i: (0, 0)),
                  pl.BlockSpec((2 * HEAD_DIM, 1), lambda b, u, i: (0, 0))],
        out_specs=step_spec(units_b),
        out_shape=fm(b_width),
        scratch_shapes=scratch(dve_b, units_b),
        compiler_params=cparams(dimension_semantics=sem),
        name="attn_b",
    )(qb, qb, kb, kb, vb, gb, lam_vecs, subln)

    ot = OUT_TILES
    assert nc % ot == 0
    out_fm_spec = lambda rows: pl.BlockSpec((1, ot, rows, t), lambda b, i: (b, i, 0, 0))
    return pl.pallas_call(
        functools.partial(_out_kernel, a_width=a_width, final=final),
        grid=(bsz, nc // ot),
        in_specs=[pl.BlockSpec((1, ot * t, d), lambda b, i: (b, i, 0)), out_fm_spec(a_width), out_fm_spec(b_width),
                  full(w_out.shape), full((1, d))],
        out_specs=pl.BlockSpec((1, ot * t, d), lambda b, i: (b, i, 0)),
        out_shape=jax.ShapeDtypeStruct(x.shape, x.dtype),
        compiler_params=cparams(dimension_semantics=("parallel", "parallel")),
        name="out_proj",
    )(x, oa, ob, w_out, g_final)


def kernel(x_prompt, x_sample, g_norm, w_in, a_q_norm, a_k_norm, b_lambda_q1, b_lambda_k1,
           b_lambda_q2, b_lambda_k2, b_subln, w_out, g_final):
    depth = w_in.shape[0]
    hp, hs = x_prompt, x_sample
    for l in range(depth):
        lam_init = 0.8 - 0.6 * math.exp(-0.3 * l)
        args = (g_norm[l][None, :], w_in[l].T.astype(BF16), a_q_norm[l][:, None], a_k_norm[l][:, None],
                jnp.stack([b_lambda_q1[l], b_lambda_k1[l], b_lambda_q2[l], b_lambda_k2[l]]).astype(F32),
                b_subln[l][:, None], w_out[l].astype(BF16), g_final[None, :])
        final = l == depth - 1
        hp = _layer(hp, *args, lam_init, final)
        hs = _layer(hs, *args, lam_init, final)
    return (hp, hs)
```

```python
import functools
import math

import jax
import jax.numpy as jnp
from jax import lax
from jax.experimental import pallas as pl
from jax.experimental.pallas import tpu as pltpu

HEAD_DIM = 64
GRID_W = 64
ROPE_THETA = 10000.0
EPS = 1e-6
TOKEN_TILE = 512
SUB_KEYS = 256
HEADS_PER_STEP = 4
STEP_ROWS = HEADS_PER_STEP * HEAD_DIM
PHASE_PAIRS = 16
PROJ_TILES = 2
OUT_TILES = 4
BF16_SUBLANES = 16
LANES = 128
KEY_LANES = 256
VMEM_LIMIT_BYTES = 56 * 1024 * 1024
LOG2E = math.log2(math.e)

F32 = jnp.float32
BF16 = jnp.bfloat16


def _dot_nt(a, b):
    return lax.dot_general(a, b, (((1,), (1,)), ((), ())), preferred_element_type=F32)


def _dot_tn(a, b):
    return lax.dot_general(a, b, (((0,), (0,)), ((), ())), preferred_element_type=F32)


def _dot(a, b):
    return jnp.dot(a, b, preferred_element_type=F32)


def _rope_axial(x, c, s):
    xs = jnp.concatenate([x[16:32], x[0:16], x[48:64], x[32:48]], axis=0)
    return x * c + xs * s


def _rope_1d(x, c, s):
    xs = jnp.concatenate([x[32:64], x[0:32]], axis=0)
    return x * c + xs * s


def _silu(g):
    return g * (1.0 / (1.0 + jnp.exp(-g)))


def _proj_kernel(x_ref, gn_ref, wt_ref, qn_ref, kn_ref, ca_ref, sa_ref, cb_ref, sb_ref,
                 qa_ref, va_ref, ga_ref, qb_ref, vb_ref, gb_ref, ka_ref, kb_ref, *, sizes):
    n_tiles, t = qa_ref.shape[1], qa_ref.shape[-1]
    for j in range(n_tiles):
        _proj_tile(j, slice(j * t, (j + 1) * t), x_ref, gn_ref, wt_ref, qn_ref, kn_ref, ca_ref, sa_ref, cb_ref,
                   sb_ref, qa_ref, va_ref, ga_ref, qb_ref, vb_ref, gb_ref, ka_ref, kb_ref, sizes)


def _proj_tile(j, tok, x_ref, gn_ref, wt_ref, qn_ref, kn_ref, ca_ref, sa_ref, cb_ref, sb_ref,
               qa_ref, va_ref, ga_ref, qb_ref, vb_ref, gb_ref, ka_ref, kb_ref, sizes):
    a_width, a_kv, b_qk, b_width = sizes
    x = x_ref[0, tok, :]
    t = x.shape[0]
    h = x * lax.rsqrt(jnp.mean(x * x, axis=-1, keepdims=True) + EPS) * gn_ref[...]
    hb = h.astype(BF16)

    def section(lo, n):
        return _dot_nt(wt_ref[lo:lo + n, :], hb)

    ca, sa, cb, sb = ca_ref[:, tok], sa_ref[:, tok], cb_ref[:, tok], sb_ref[:, tok]
    qscale = HEAD_DIM ** -0.5 * LOG2E
    ones_rows = (lax.broadcasted_iota(jnp.int32, (BF16_SUBLANES, t), 0) == 0).astype(BF16)

    def head_norm(blk, g):
        return blk * lax.rsqrt(jnp.mean(blk * blk, axis=0, keepdims=True) + EPS) * g

    def store_values(v_ref, vals, dv):
        dve = dv + BF16_SUBLANES
        for hd in range(vals.shape[0] // dv):
            v_ref[0, j, hd * dve:hd * dve + dv, :] = vals[hd * dv:(hd + 1) * dv].astype(BF16)
            v_ref[0, j, hd * dve + dv:(hd + 1) * dve, :] = ones_rows

    off = 0
    r = section(off, a_width)
    for hd in range(a_width // HEAD_DIM):
        blk = r[hd * HEAD_DIM:(hd + 1) * HEAD_DIM]
        y = _rope_axial(head_norm(blk, qn_ref[...]), ca, sa) * qscale
        qa_ref[0, j, hd * HEAD_DIM:(hd + 1) * HEAD_DIM, :] = y.astype(BF16)
    off += a_width

    r = section(off, 2 * a_kv)
    ks = []
    for hd in range(a_kv // HEAD_DIM):
        blk = r[hd * HEAD_DIM:(hd + 1) * HEAD_DIM]
        ks.append(_rope_axial(head_norm(blk, kn_ref[...]), ca, sa))
    ka_ref[0, tok, 0:a_kv] = jnp.concatenate(ks, axis=0).T.astype(BF16)
    ka_ref[0, tok, a_kv:] = jnp.zeros((t, ka_ref.shape[-1] - a_kv), BF16)
    store_values(va_ref, r[a_kv:2 * a_kv], HEAD_DIM)
    off += 2 * a_kv

    ga_ref[0, j] = _silu(section(off, a_width)).astype(BF16)
    off += a_width

    r = section(off, b_qk)
    for hd in range(b_qk // HEAD_DIM):
        blk = r[hd * HEAD_DIM:(hd + 1) * HEAD_DIM]
        qb_ref[0, j, hd * HEAD_DIM:(hd + 1) * HEAD_DIM, :] = (_rope_1d(blk, cb, sb) * qscale).astype(BF16)
    off += b_qk

    r = section(off, b_qk)
    ks = [_rope_1d(r[hd * HEAD_DIM:(hd + 1) * HEAD_DIM], cb, sb) for hd in range(b_qk // HEAD_DIM)]
    kb_ref[0, tok, :] = jnp.concatenate(ks, axis=0).T.astype(BF16)
    off += b_qk

    store_values(vb_ref, section(off, b_width), 2 * HEAD_DIM)
    off += b_width
    gb_ref[0, j] = _silu(section(off, b_width)).astype(BF16)


def _attn_kernel(*refs, mode, n_chunks, lam_init, grid_dims, group, units):
    n_s = 2 * group
    if mode == "A":
        q_ref, qn_ref, k_ref, kn_ref, v_ref, g_ref, o_ref = refs[:7]
        dv = HEAD_DIM
    else:
        q_ref, qn_ref, k_ref, kn_ref, v_ref, g_ref, lam_ref, sub_ref, o_ref = refs[:9]
        dv = 2 * HEAD_DIM
    s_refs, (acc_ref, mcar_ref) = refs[-2 - n_s:-2], refs[-2:]
    dve = dv + BF16_SUBLANES
    nq, tq, tk = q_ref.shape[1], q_ref.shape[-1], v_ref.shape[-1]
    n_maps = units * nq * HEADS_PER_STEP
    n_phases = n_maps // group
    assert n_maps % group == 0 and n_phases % 2 == 0
    slots = k_ref.shape[-1] // HEAD_DIM
    n_sub = tk // SUB_KEYS
    _, nu, ni = grid_dims
    b, u, i = pl.program_id(0), pl.program_id(1), pl.program_id(2)
    is_first = (b == 0) & (u == 0) & (i == 0)
    u_next = (u + (i + 1) // ni) % nu
    z = jnp.zeros((HEAD_DIM, tq), BF16)

    def split(m):
        ul, rest = divmod(m, nq * HEADS_PER_STEP)
        tl, hd = divmod(rest, HEADS_PER_STEP)
        return ul, tl, hd

    def place(rows64, slot):
        rows = [z] * slots
        rows[slot] = rows64
        return jnp.concatenate(rows, axis=0)

    def expand(rows64, hd, step_unit, ul):
        if mode == "B":
            return place(rows64, hd)
        if nu == 1:
            return place(rows64, ul)
        return jnp.where(step_unit * units + ul == 0, place(rows64, 0), place(rows64, 1))

    def q_rows(ul, hd):
        return slice(ul * STEP_ROWS + hd * HEAD_DIM, ul * STEP_ROWS + (hd + 1) * HEAD_DIM)

    qx = []
    for m in range(n_maps):
        ul, tl, hd = split(m)
        qx.append(expand(q_ref[0, tl, q_rows(ul, hd), :], hd, u, ul))
    qx_next = [expand(qn_ref[0, 0, q_rows(0, m), :], m, u_next, 0) for m in range(group)]

    def v_rows(m):
        ul, _, hd = split(m)
        blk = ul if mode == "A" else ul * (HEADS_PER_STEP // 2) + hd // 2
        return slice(blk * dve, (blk + 1) * dve)

    def keys(c, j):
        return pl.ds(c * tk + j * SUB_KEYS, SUB_KEYS)

    def pass1(kref, qxm, s_dst, c, j, m8):
        s = _dot(kref[0, keys(c, j), :], qxm)
        s_dst[keys(c, j), :] = s
        return jnp.maximum(m8, jnp.max(s.reshape(SUB_KEYS // 8, 8, tq), axis=0))

    def pass2(m, s_src, mb, c, j):
        s = s_src[keys(c, j), :]
        x = (s.reshape(SUB_KEYS // 8, 8, tq) - mb[None]).reshape(SUB_KEYS, tq).astype(BF16)
        p = jnp.exp2(x)
        vs = v_ref[0, c, v_rows(m), j * SUB_KEYS:(j + 1) * SUB_KEYS]
        return _dot(vs, p)

    neg = jnp.full((8, tq), -jnp.inf, F32)

    def run_phase(trips, p1s, p2s):
        def phase(_, m8s):
            m8s = list(m8s)
            for c in range(n_chunks):
                parts = [None] * len(p2s)
                for j in range(n_sub):
                    for k in range(max(len(p1s), len(p2s))):
                        if k < len(p1s):
                            m8s[k] = pass1(*p1s[k], c, j, m8s[k])
                        if k < len(p2s):
                            d = pass2(*p2s[k], c, j)
                            parts[k] = d if parts[k] is None else parts[k] + d
                for k, p2 in enumerate(p2s):
                    if c == 0:
                        acc_ref[p2[0]] = parts[k]
                    else:
                        acc_ref[p2[0]] += parts[k]
            return tuple(m8s)

        return lax.fori_loop(0, trips, phase, (neg,) * len(p1s))

    def col_max(m8):
        return jnp.broadcast_to(jnp.max(m8, axis=0, keepdims=True), (8, tq))

    def s_buf(parity, k):
        return s_refs[parity * group + k]

    m_first = run_phase(is_first.astype(jnp.int32), [(k_ref, qx[k], s_buf(0, k)) for k in range(group)], [])

    @pl.when(is_first)
    def _():
        for k in range(group):
            mcar_ref[k] = m_first[k]

    once = jnp.minimum(i, 0) + 1
    m8s = [mcar_ref[k] for k in range(group)]
    for ph in range(1, n_phases + 1):
        if ph < n_phases:
            p1s = [(k_ref, qx[ph * group + k], s_buf(ph % 2, k)) for k in range(group)]
        else:
            p1s = [(kn_ref, qx_next[k], s_buf(0, k)) for k in range(group)]
        p2s = [((ph - 1) * group + k, s_buf((ph - 1) % 2, k), col_max(m8s[k])) for k in range(group)]
        m8s = run_phase(once, p1s, p2s)
    for k in range(group):
        mcar_ref[k] = m8s[k]

    def normalised(m, lanes):
        return acc_ref[m, 0:dv, lanes] * (1.0 / acc_ref[m, dv:dv + 1, lanes])

    if mode == "B":
        lv = lam_ref[...]
        lam = (jnp.exp(jnp.sum(lv[0:1] * lv[1:2], axis=-1, keepdims=True))
               - jnp.exp(jnp.sum(lv[2:3] * lv[3:4], axis=-1, keepdims=True)) + lam_init)
        sub_gain = sub_ref[...] * (1.0 - lam_init)
    for ul in range(units):
        for t_idx in range(nq):
            base = (ul * nq + t_idx) * HEADS_PER_STEP
            for strip in range(tq // LANES):
                lanes = slice(strip * LANES, (strip + 1) * LANES)
                for j in range(STEP_ROWS // dv):
                    rows = slice(ul * STEP_ROWS + j * dv, ul * STEP_ROWS + (j + 1) * dv)
                    g = g_ref[0, t_idx, rows, lanes].astype(F32)
                    if mode == "A":
                        y = normalised(base + j, lanes)
                    else:
                        o = normalised(base + 2 * j, lanes) - lam * normalised(base + 2 * j + 1, lanes)
                        y = o * lax.rsqrt(jnp.mean(o * o, axis=0, keepdims=True) + EPS) * sub_gain
                    o_ref[0, t_idx, rows, lanes] = (y * g).astype(BF16)


def _out_kernel(x_ref, oa_ref, ob_ref, w_ref, gf_ref, y_ref, *, a_width, final):
    n_tiles, t = oa_ref.shape[1], oa_ref.shape[-1]
    for j in range(n_tiles):
        tok = slice(j * t, (j + 1) * t)
        y = (x_ref[0, tok, :] + _dot_tn(oa_ref[0, j], w_ref[0:a_width, :])
             + _dot_tn(ob_ref[0, j], w_ref[a_width:, :]))
        if final:
            y = y * lax.rsqrt(jnp.mean(y * y, axis=-1, keepdims=True) + EPS) * gf_ref[...]
        y_ref[0, tok, :] = y


def _rope_tables(seq):
    def cs(pos, dim):
        inv = ROPE_THETA ** (-jnp.arange(0, dim, 2, dtype=F32) / dim)
        ang = pos[None, :] * inv[:, None]
        return jnp.cos(ang), jnp.sin(ang)

    rows = seq // GRID_W
    row = jnp.repeat(jnp.arange(rows, dtype=F32), GRID_W)
    col = jnp.tile(jnp.arange(GRID_W, dtype=F32), rows)
    pos = jnp.arange(seq, dtype=F32)
    cr, sr = cs(row, HEAD_DIM // 2)
    cc, sc = cs(col, HEAD_DIM // 2)
    cp, sp = cs(pos, HEAD_DIM)
    ca = jnp.concatenate([cr, cr, cc, cc], axis=0)
    sa = jnp.concatenate([-sr, sr, -sc, sc], axis=0)
    cb = jnp.concatenate([cp, cp], axis=0)
    sb = jnp.concatenate([-sp, sp], axis=0)
    return ca, sa, cb, sb


def _layer(x, gn, wt, qn, kn, lam_vecs, subln, w_out, g_final, lam_init, final):
    bsz, seq, d = x.shape
    a_width = (d // (2 * HEAD_DIM)) * HEAD_DIM
    a_kv = max(1, (a_width // HEAD_DIM) // 4) * HEAD_DIM
    b_qk = (d // (4 * HEAD_DIM)) * 2 * HEAD_DIM
    b_width = b_qk
    assert a_width == HEADS_PER_STEP * a_kv and a_width % STEP_ROWS == 0 and b_width % STEP_ROWS == 0
    t = TOKEN_TILE
    assert seq % t == 0 and seq % GRID_W == 0
    nc = seq // t
    va_rows = (a_kv // HEAD_DIM) * (HEAD_DIM + BF16_SUBLANES)
    vb_rows = (b_width // (2 * HEAD_DIM)) * (2 * HEAD_DIM + BF16_SUBLANES)
    ca, sa, cb, sb = _rope_tables(seq)
    cparams = functools.partial(pltpu.CompilerParams, vmem_limit_bytes=VMEM_LIMIT_BYTES)

    def fm(rows):
        return jax.ShapeDtypeStruct((bsz, nc, rows, t), BF16)

    pt = PROJ_TILES
    assert nc % pt == 0
    fm_spec = lambda rows: pl.BlockSpec((1, pt, rows, t), lambda b, i: (b, i, 0, 0))
    tab_spec = pl.BlockSpec((HEAD_DIM, pt * t), lambda b, i: (0, i))
    full = lambda shape: pl.BlockSpec(shape, lambda b, i: (0,) * len(shape))

    qa, va, ga, qb, vb, gb, ka, kb = pl.pallas_call(
        functools.partial(_proj_kernel, sizes=(a_width, a_kv, b_qk, b_width)),
        grid=(bsz, nc // pt),
        in_specs=[pl.BlockSpec((1, pt * t, d), lambda b, i: (b, i, 0)), full((1, d)), full(wt.shape),
                  full((HEAD_DIM, 1)), full((HEAD_DIM, 1)), tab_spec, tab_spec, tab_spec, tab_spec],
        out_specs=[fm_spec(a_width), fm_spec(va_rows), fm_spec(a_width), fm_spec(b_qk), fm_spec(vb_rows),
                   fm_spec(b_width),
                   pl.BlockSpec((1, pt * t, KEY_LANES), lambda b, i: (b, i, 0)),
                   pl.BlockSpec((1, pt * t, b_qk), lambda b, i: (b, i, 0))],
        out_shape=[fm(a_width), fm(va_rows), fm(a_width), fm(b_qk), fm(vb_rows), fm(b_width),
                   jax.ShapeDtypeStruct((bsz, seq, KEY_LANES), BF16),
                   jax.ShapeDtypeStruct((bsz, seq, b_qk), BF16)],
        compiler_params=cparams(dimension_semantics=("parallel", "parallel")),
        name="proj",
    )(x, gn, wt, qn, kn, ca, sa, cb, sb)

    group = max(1, min(HEADS_PER_STEP, PHASE_PAIRS // nc))
    nq = max(1, 2 * group // HEADS_PER_STEP)
    assert nc % nq == 0
    units_a = a_width // STEP_ROWS
    units_b = 1
    grid_a = (bsz, a_width // STEP_ROWS // units_a, nc // nq)
    grid_b = (bsz, b_width // STEP_ROWS // units_b, nc // nq)

    def next_step(dims):
        nb, nu, ni = dims

        def f(b, u, i):
            cu = (i + 1) // ni
            cb = (u + cu) // nu
            return jnp.minimum(b + cb, nb - 1), (u + cu) % nu, (i + 1) % ni
        return f

    step_spec = lambda units: pl.BlockSpec((1, nq, units * STEP_ROWS, t), lambda b, u, i: (b, i, u, 0))
    sem = ("arbitrary", "arbitrary", "arbitrary")

    def next_q_spec(dims, units):
        nxt = next_step(dims)
        rows = group * HEAD_DIM

        def index(b, u, i):
            b2, u2, i2 = nxt(b, u, i)
            return b2, i2 * nq, u2 * (units * STEP_ROWS // rows), 0
        return pl.BlockSpec((1, 1, rows, t), index)

    def scratch(dve, units):
        return ([pltpu.VMEM((seq, t), F32)] * (2 * group)
                + [pltpu.VMEM((units * nq * HEADS_PER_STEP, dve, t), F32), pltpu.VMEM((group, 8, t), F32)])

    dve_a = HEAD_DIM + BF16_SUBLANES
    ka_spec = lambda index: pl.BlockSpec((1, seq, KEY_LANES), index)
    oa = pl.pallas_call(
        functools.partial(_attn_kernel, mode="A", n_chunks=nc, lam_init=lam_init, grid_dims=grid_a, group=group,
                          units=units_a),
        grid=grid_a,
        in_specs=[step_spec(units_a), next_q_spec(grid_a, units_a),
                  ka_spec(lambda b, u, i: (b, 0, 0)),
                  ka_spec(lambda b, u, i: (next_step(grid_a)(b, u, i)[0], 0, 0)),
                  pl.BlockSpec((1, nc, units_a * dve_a, t), lambda b, u, i: (b, 0, u, 0)),
                  step_spec(units_a)],
        out_specs=step_spec(units_a),
        out_shape=fm(a_width),
        scratch_shapes=scratch(dve_a, units_a),
        compiler_params=cparams(dimension_semantics=sem),
        name="attn_a",
    )(qa, qa, ka, ka, va, ga)

    dve_b = 2 * HEAD_DIM + BF16_SUBLANES
    kb_spec = lambda index: pl.BlockSpec((1, seq, STEP_ROWS), index)

    def kb_next(b, u, i):
        b2, u2, _ = next_step(grid_b)(b, u, i)
        return b2, 0, u2

    ob = pl.pallas_call(
        functools.partial(_attn_kernel, mode="B", n_chunks=nc, lam_init=lam_init, grid_dims=grid_b, group=group,
                          units=units_b),
        grid=grid_b,
        in_specs=[step_spec(units_b), next_q_spec(grid_b, units_b),
                  kb_spec(lambda b, u, i: (b, 0, u)),
                  kb_spec(kb_next),
                  pl.BlockSpec((1, nc, 2 * dve_b, t), lambda b, u, i: (b, 0, u, 0)),
                  step_spec(units_b),
                  pl.BlockSpec((4, HEAD_DIM), lambda b, u, i: (0, 0)),
                  pl.BlockSpec((2 * HEAD_DIM, 1), lambda b, u, i: (0, 0))],
        out_specs=step_spec(units_b),
        out_shape=fm(b_width),
        scratch_shapes=scratch(dve_b, units_b),
        compiler_params=cparams(dimension_semantics=sem),
        name="attn_b",
    )(qb, qb, kb, kb, vb, gb, lam_vecs, subln)

    ot = OUT_TILES
    assert nc % ot == 0
    out_fm_spec = lambda rows: pl.BlockSpec((1, ot, rows, t), lambda b, i: (b, i, 0, 0))
    return pl.pallas_call(
        functools.partial(_out_kernel, a_width=a_width, final=final),
        grid=(bsz, nc // ot),
        in_specs=[pl.BlockSpec((1, ot * t, d), lambda b, i: (b, i, 0)), out_fm_spec(a_width), out_fm_spec(b_width),
                  full(w_out.shape), full((1, d))],
        out_specs=pl.BlockSpec((1, ot * t, d), lambda b, i: (b, i, 0)),
        out_shape=jax.ShapeDtypeStruct(x.shape, x.dtype),
        compiler_params=cparams(dimension_semantics=("parallel", "parallel")),
        name="out_proj",
    )(x, oa, ob, w_out, g_final)


def kernel(x_prompt, x_sample, g_norm, w_in, a_q_norm, a_k_norm, b_lambda_q1, b_lambda_k1,
           b_lambda_q2, b_lambda_k2, b_subln, w_out, g_final):
    depth = w_in.shape[0]
    hp, hs = x_prompt, x_sample
    for l in range(depth):
        lam_init = 0.8 - 0.6 * math.exp(-0.3 * l)
        args = (g_norm[l][None, :], w_in[l].T.astype(BF16), a_q_norm[l][:, None], a_k_norm[l][:, None],
                jnp.stack([b_lambda_q1[l], b_lambda_k1[l], b_lambda_q2[l], b_lambda_k2[l]]).astype(F32),
                b_subln[l][:, None], w_out[l].astype(BF16), g_final[None, :])
        final = l == depth - 1
        hp = _layer(hp, *args, lam_init, final)
        hs = _layer(hs, *args, lam_init, final)
    return (hp, hs)
```

```python
import functools
import math

import jax
import jax.numpy as jnp
from jax import lax
from jax.experimental import pallas as pl
from jax.experimental.pallas import tpu as pltpu

HEAD_DIM = 64
GRID_W = 64
ROPE_THETA = 10000.0
EPS = 1e-6
TOKEN_TILE = 512
SUB_KEYS = 256
HEADS_PER_STEP = 4
STEP_ROWS = HEADS_PER_STEP * HEAD_DIM
PHASE_PAIRS = 16
PROJ_TILES = 2
OUT_TILES = 4
BF16_SUBLANES = 16
LANES = 128
KEY_LANES = 256
VMEM_LIMIT_BYTES = 56 * 1024 * 1024
LOG2E = math.log2(math.e)

F32 = jnp.float32
BF16 = jnp.bfloat16


def _dot_nt(a, b):
    return lax.dot_general(a, b, (((1,), (1,)), ((), ())), preferred_element_type=F32)


def _dot_tn(a, b):
    return lax.dot_general(a, b, (((0,), (0,)), ((), ())), preferred_element_type=F32)


def _dot(a, b):
    return jnp.dot(a, b, preferred_element_type=F32)


def _rope_axial(x, c, s):
    xs = jnp.concatenate([x[16:32], x[0:16], x[48:64], x[32:48]], axis=0)
    return x * c + xs * s


def _rope_1d(x, c, s):
    xs = jnp.concatenate([x[32:64], x[0:32]], axis=0)
    return x * c + xs * s


def _silu(g):
    return g * (1.0 / (1.0 + jnp.exp(-g)))


def _proj_kernel(x_ref, gn_ref, wt_ref, qn_ref, kn_ref, ca_ref, sa_ref, cb_ref, sb_ref,
                 qa_ref, va_ref, ga_ref, qb_ref, vb_ref, gb_ref, ka_ref, kb_ref, *, sizes):
    n_tiles, t = qa_ref.shape[1], qa_ref.shape[-1]
    for j in range(n_tiles):
        _proj_tile(j, slice(j * t, (j + 1) * t), x_ref, gn_ref, wt_ref, qn_ref, kn_ref, ca_ref, sa_ref, cb_ref,
                   sb_ref, qa_ref, va_ref, ga_ref, qb_ref, vb_ref, gb_ref, ka_ref, kb_ref, sizes)


def _proj_tile(j, tok, x_ref, gn_ref, wt_ref, qn_ref, kn_ref, ca_ref, sa_ref, cb_ref, sb_ref,
               qa_ref, va_ref, ga_ref, qb_ref, vb_ref, gb_ref, ka_ref, kb_ref, sizes):
    a_width, a_kv, b_qk, b_width = sizes
    x = x_ref[0, tok, :]
    t = x.shape[0]
    h = x * lax.rsqrt(jnp.mean(x * x, axis=-1, keepdims=True) + EPS) * gn_ref[...]
    hb = h.astype(BF16)

    def section(lo, n):
        return _dot_nt(wt_ref[lo:lo + n, :], hb)

    ca, sa, cb, sb = ca_ref[:, tok], sa_ref[:, tok], cb_ref[:, tok], sb_ref[:, tok]
    qscale = HEAD_DIM ** -0.5 * LOG2E
    ones_rows = (lax.broadcasted_iota(jnp.int32, (BF16_SUBLANES, t), 0) == 0).astype(BF16)

    def head_norm(blk, g):
        return blk * lax.rsqrt(jnp.mean(blk * blk, axis=0, keepdims=True) + EPS) * g

    def store_values(v_ref, vals, dv):
        dve = dv + BF16_SUBLANES
        for hd in range(vals.shape[0] // dv):
            v_ref[0, j, hd * dve:hd * dve + dv, :] = vals[hd * dv:(hd + 1) * dv].astype(BF16)
            v_ref[0, j, hd * dve + dv:(hd + 1) * dve, :] = ones_rows

    off = 0
    r = section(off, a_width)
    for hd in range(a_width // HEAD_DIM):
        blk = r[hd * HEAD_DIM:(hd + 1) * HEAD_DIM]
        y = _rope_axial(head_norm(blk, qn_ref[...]), ca, sa) * qscale
        qa_ref[0, j, hd * HEAD_DIM:(hd + 1) * HEAD_DIM, :] = y.astype(BF16)
    off += a_width

    r = section(off, 2 * a_kv)
    ks = []
    for hd in range(a_kv // HEAD_DIM):
        blk = r[hd * HEAD_DIM:(hd + 1) * HEAD_DIM]
        ks.append(_rope_axial(head_norm(blk, kn_ref[...]), ca, sa))
    ka_ref[0, tok, 0:a_kv] = jnp.concatenate(ks, axis=0).T.astype(BF16)
    ka_ref[0, tok, a_kv:] = jnp.zeros((t, ka_ref.shape[-1] - a_kv), BF16)
    store_values(va_ref, r[a_kv:2 * a_kv], HEAD_DIM)
    off += 2 * a_kv

    ga_ref[0, j] = _silu(section(off, a_width)).astype(BF16)
    off += a_width

    r = section(off, b_qk)
    for hd in range(b_qk // HEAD_DIM):
        blk = r[hd * HEAD_DIM:(hd + 1) * HEAD_DIM]
        qb_ref[0, j, hd * HEAD_DIM:(hd + 1) * HEAD_DIM, :] = (_rope_1d(blk, cb, sb) * qscale).astype(BF16)
    off += b_qk

    r = section(off, b_qk)
    ks = [_rope_1d(r[hd * HEAD_DIM:(hd + 1) * HEAD_DIM], cb, sb) for hd in range(b_qk // HEAD_DIM)]
    kb_ref[0, tok, :] = jnp.concatenate(ks, axis=0).T.astype(BF16)
    off += b_qk

    store_values(vb_ref, section(off, b_width), 2 * HEAD_DIM)
    off += b_width
    gb_ref[0, j] = _silu(section(off, b_width)).astype(BF16)


def _attn_kernel(*refs, mode, n_chunks, lam_init, grid_dims, group, units):
    n_s = 2 * group
    if mode == "A":
        q_ref, qn_ref, k_ref, kn_ref, v_ref, g_ref, o_ref = refs[:7]
        dv = HEAD_DIM
    else:
        q_ref, qn_ref, k_ref, kn_ref, v_ref, g_ref, lam_ref, sub_ref, o_ref = refs[:9]
        dv = 2 * HEAD_DIM
    s_refs, (acc_ref, mcar_ref) = refs[-2 - n_s:-2], refs[-2:]
    dve = dv + BF16_SUBLANES
    nq, tq, tk = q_ref.shape[1], q_ref.shape[-1], v_ref.shape[-1]
    n_maps = units * nq * HEADS_PER_STEP
    n_phases = n_maps // group
    assert n_maps % group == 0 and n_phases % 2 == 0
    slots = k_ref.shape[-1] // HEAD_DIM
    n_sub = tk // SUB_KEYS
    _, nu, ni = grid_dims
    b, u, i = pl.program_id(0), pl.program_id(1), pl.program_id(2)
    is_first = (b == 0) & (u == 0) & (i == 0)
    u_next = (u + (i + 1) // ni) % nu
    z = jnp.zeros((HEAD_DIM, tq), BF16)

    def split(m):
        ul, rest = divmod(m, nq * HEADS_PER_STEP)
        tl, hd = divmod(rest, HEADS_PER_STEP)
        return ul, tl, hd

    def place(rows64, slot):
        rows = [z] * slots
        rows[slot] = rows64
        return jnp.concatenate(rows, axis=0)

    def expand(rows64, hd, step_unit, ul):
        if mode == "B":
            return place(rows64, hd)
        if nu == 1:
            return place(rows64, ul)
        return jnp.where(step_unit * units + ul == 0, place(rows64, 0), place(rows64, 1))

    def q_rows(ul, hd):
        return slice(ul * STEP_ROWS + hd * HEAD_DIM, ul * STEP_ROWS + (hd + 1) * HEAD_DIM)

    qx = []
    for m in range(n_maps):
        ul, tl, hd = split(m)
        qx.append(expand(q_ref[0, tl, q_rows(ul, hd), :], hd, u, ul))
    qx_next = [expand(qn_ref[0, 0, q_rows(0, m), :], m, u_next, 0) for m in range(group)]

    def v_rows(m):
        ul, _, hd = split(m)
        blk = ul if mode == "A" else ul * (HEADS_PER_STEP // 2) + hd // 2
        return slice(blk * dve, (blk + 1) * dve)

    def keys(c, j):
        return pl.ds(c * tk + j * SUB_KEYS, SUB_KEYS)

    def pass1(kref, qxm, s_dst, c, j, m8):
        s = _dot(kref[0, keys(c, j), :], qxm)
        s_dst[keys(c, j), :] = s
        return jnp.maximum(m8, jnp.max(s.reshape(SUB_KEYS // 8, 8, tq), axis=0))

    def pass2(m, s_src, mb, c, j):
        s = s_src[keys(c, j), :]
        p = jnp.exp2(s.reshape(SUB_KEYS // 8, 8, tq) - mb[None]).reshape(SUB_KEYS, tq).astype(BF16)
        vs = v_ref[0, c, v_rows(m), j * SUB_KEYS:(j + 1) * SUB_KEYS]
        return _dot(vs, p)

    neg = jnp.full((8, tq), -jnp.inf, F32)

    def normalised(m, lanes):
        return acc_ref[m, 0:dv, lanes] * (1.0 / acc_ref[m, dv:dv + 1, lanes])

    if mode == "B":
        lv = lam_ref[...]
        lam = (jnp.exp(jnp.sum(lv[0:1] * lv[1:2], axis=-1, keepdims=True))
               - jnp.exp(jnp.sum(lv[2:3] * lv[3:4], axis=-1, keepdims=True)) + lam_init)
        sub_gain = sub_ref[...] * (1.0 - lam_init)

    def finish(map_ids):
        for m in map_ids[::dv // HEAD_DIM]:
            ul, t_idx, hd = split(m)
            rows = slice(ul * STEP_ROWS + hd * HEAD_DIM, ul * STEP_ROWS + hd * HEAD_DIM + dv)
            for strip in range(tq // LANES):
                lanes = slice(strip * LANES, (strip + 1) * LANES)
                g = g_ref[0, t_idx, rows, lanes].astype(F32)
                if mode == "A":
                    y = normalised(m, lanes)
                else:
                    o = normalised(m, lanes) - lam * normalised(m + 1, lanes)
                    y = o * lax.rsqrt(jnp.mean(o * o, axis=0, keepdims=True) + EPS) * sub_gain
                o_ref[0, t_idx, rows, lanes] = (y * g).astype(BF16)

    def run_phase(trips, p1s, p2s, done=()):
        def phase(_, m8s):
            m8s = list(m8s)
            finish(done)
            for c in range(n_chunks):
                parts = [None] * len(p2s)
                for j in range(n_sub):
                    for k in range(max(len(p1s), len(p2s))):
                        if k < len(p1s):
                            m8s[k] = pass1(*p1s[k], c, j, m8s[k])
                        if k < len(p2s):
                            d = pass2(*p2s[k], c, j)
                            parts[k] = d if parts[k] is None else parts[k] + d
                for k, p2 in enumerate(p2s):
                    if c == 0:
                        acc_ref[p2[0]] = parts[k]
                    else:
                        acc_ref[p2[0]] += parts[k]
            return tuple(m8s)

        return lax.fori_loop(0, trips, phase, (neg,) * len(p1s))

    def col_max(m8):
        return jnp.broadcast_to(jnp.max(m8, axis=0, keepdims=True), (8, tq))

    def s_buf(parity, k):
        return s_refs[parity * group + k]

    m_first = run_phase(is_first.astype(jnp.int32), [(k_ref, qx[k], s_buf(0, k)) for k in range(group)], [])

    @pl.when(is_first)
    def _():
        for k in range(group):
            mcar_ref[k] = m_first[k]

    once = jnp.minimum(i, 0) + 1
    m8s = [mcar_ref[k] for k in range(group)]
    for ph in range(1, n_phases + 1):
        if ph < n_phases:
            p1s = [(k_ref, qx[ph * group + k], s_buf(ph % 2, k)) for k in range(group)]
        else:
            p1s = [(kn_ref, qx_next[k], s_buf(0, k)) for k in range(group)]
        p2s = [((ph - 1) * group + k, s_buf((ph - 1) % 2, k), col_max(m8s[k])) for k in range(group)]
        done = tuple(range((ph - 2) * group, (ph - 1) * group)) if ph >= 2 else ()
        m8s = run_phase(once, p1s, p2s, done)
    for k in range(group):
        mcar_ref[k] = m8s[k]
    finish(tuple(range((n_phases - 1) * group, n_maps)))


def _out_kernel(x_ref, oa_ref, ob_ref, w_ref, gf_ref, y_ref, *, a_width, final):
    n_tiles, t = oa_ref.shape[1], oa_ref.shape[-1]
    for j in range(n_tiles):
        tok = slice(j * t, (j + 1) * t)
        y = (x_ref[0, tok, :] + _dot_tn(oa_ref[0, j], w_ref[0:a_width, :])
             + _dot_tn(ob_ref[0, j], w_ref[a_width:, :]))
        if final:
            y = y * lax.rsqrt(jnp.mean(y * y, axis=-1, keepdims=True) + EPS) * gf_ref[...]
        y_ref[0, tok, :] = y


def _rope_tables(seq):
    def cs(pos, dim):
        inv = ROPE_THETA ** (-jnp.arange(0, dim, 2, dtype=F32) / dim)
        ang = pos[None, :] * inv[:, None]
        return jnp.cos(ang), jnp.sin(ang)

    rows = seq // GRID_W
    row = jnp.repeat(jnp.arange(rows, dtype=F32), GRID_W)
    col = jnp.tile(jnp.arange(GRID_W, dtype=F32), rows)
    pos = jnp.arange(seq, dtype=F32)
    cr, sr = cs(row, HEAD_DIM // 2)
    cc, sc = cs(col, HEAD_DIM // 2)
    cp, sp = cs(pos, HEAD_DIM)
    ca = jnp.concatenate([cr, cr, cc, cc], axis=0)
    sa = jnp.concatenate([-sr, sr, -sc, sc], axis=0)
    cb = jnp.concatenate([cp, cp], axis=0)
    sb = jnp.concatenate([-sp, sp], axis=0)
    return ca, sa, cb, sb


def _layer(x, gn, wt, qn, kn, lam_vecs, subln, w_out, g_final, lam_init, final):
    bsz, seq, d = x.shape
    a_width = (d // (2 * HEAD_DIM)) * HEAD_DIM
    a_kv = max(1, (a_width // HEAD_DIM) // 4) * HEAD_DIM
    b_qk = (d // (4 * HEAD_DIM)) * 2 * HEAD_DIM
    b_width = b_qk
    assert a_width == HEADS_PER_STEP * a_kv and a_width % STEP_ROWS == 0 and b_width % STEP_ROWS == 0
    t = TOKEN_TILE
    assert seq % t == 0 and seq % GRID_W == 0
    nc = seq // t
    va_rows = (a_kv // HEAD_DIM) * (HEAD_DIM + BF16_SUBLANES)
    vb_rows = (b_width // (2 * HEAD_DIM)) * (2 * HEAD_DIM + BF16_SUBLANES)
    ca, sa, cb, sb = _rope_tables(seq)
    cparams = functools.partial(pltpu.CompilerParams, vmem_limit_bytes=VMEM_LIMIT_BYTES)

    def fm(rows):
        return jax.ShapeDtypeStruct((bsz, nc, rows, t), BF16)

    pt = PROJ_TILES
    assert nc % pt == 0
    fm_spec = lambda rows: pl.BlockSpec((1, pt, rows, t), lambda b, i: (b, i, 0, 0))
    tab_spec = pl.BlockSpec((HEAD_DIM, pt * t), lambda b, i: (0, i))
    full = lambda shape: pl.BlockSpec(shape, lambda b, i: (0,) * len(shape))

    qa, va, ga, qb, vb, gb, ka, kb = pl.pallas_call(
        functools.partial(_proj_kernel, sizes=(a_width, a_kv, b_qk, b_width)),
        grid=(bsz, nc // pt),
        in_specs=[pl.BlockSpec((1, pt * t, d), lambda b, i: (b, i, 0)), full((1, d)), full(wt.shape),
                  full((HEAD_DIM, 1)), full((HEAD_DIM, 1)), tab_spec, tab_spec, tab_spec, tab_spec],
        out_specs=[fm_spec(a_width), fm_spec(va_rows), fm_spec(a_width), fm_spec(b_qk), fm_spec(vb_rows),
                   fm_spec(b_width),
                   pl.BlockSpec((1, pt * t, KEY_LANES), lambda b, i: (b, i, 0)),
                   pl.BlockSpec((1, pt * t, b_qk), lambda b, i: (b, i, 0))],
        out_shape=[fm(a_width), fm(va_rows), fm(a_width), fm(b_qk), fm(vb_rows), fm(b_width),
                   jax.ShapeDtypeStruct((bsz, seq, KEY_LANES), BF16),
                   jax.ShapeDtypeStruct((bsz, seq, b_qk), BF16)],
        compiler_params=cparams(dimension_semantics=("parallel", "parallel")),
        name="proj",
    )(x, gn, wt, qn, kn, ca, sa, cb, sb)

    group = max(1, min(HEADS_PER_STEP, PHASE_PAIRS // nc))
    nq = max(1, 2 * group // HEADS_PER_STEP)
    assert nc % nq == 0
    units_a = a_width // STEP_ROWS
    units_b = 1
    grid_a = (bsz, a_width // STEP_ROWS // units_a, nc // nq)
    grid_b = (bsz, b_width // STEP_ROWS // units_b, nc // nq)

    def next_step(dims):
        nb, nu, ni = dims

        def f(b, u, i):
            cu = (i + 1) // ni
            cb = (u + cu) // nu
            return jnp.minimum(b + cb, nb - 1), (u + cu) % nu, (i + 1) % ni
        return f

    step_spec = lambda units: pl.BlockSpec((1, nq, units * STEP_ROWS, t), lambda b, u, i: (b, i, u, 0))
    sem = ("arbitrary", "arbitrary", "arbitrary")

    def next_q_spec(dims, units):
        nxt = next_step(dims)
        rows = group * HEAD_DIM

        def index(b, u, i):
            b2, u2, i2 = nxt(b, u, i)
            return b2, i2 * nq, u2 * (units * STEP_ROWS // rows), 0
        return pl.BlockSpec((1, 1, rows, t), index)

    def scratch(dve, units):
        return ([pltpu.VMEM((seq, t), F32)] * (2 * group)
                + [pltpu.VMEM((units * nq * HEADS_PER_STEP, dve, t), F32), pltpu.VMEM((group, 8, t), F32)])

    dve_a = HEAD_DIM + BF16_SUBLANES
    ka_spec = lambda index: pl.BlockSpec((1, seq, KEY_LANES), index)
    oa = pl.pallas_call(
        functools.partial(_attn_kernel, mode="A", n_chunks=nc, lam_init=lam_init, grid_dims=grid_a, group=group,
                          units=units_a),
        grid=grid_a,
        in_specs=[step_spec(units_a), next_q_spec(grid_a, units_a),
                  ka_spec(lambda b, u, i: (b, 0, 0)),
                  ka_spec(lambda b, u, i: (next_step(grid_a)(b, u, i)[0], 0, 0)),
                  pl.BlockSpec((1, nc, units_a * dve_a, t), lambda b, u, i: (b, 0, u, 0)),
                  step_spec(units_a)],
        out_specs=step_spec(units_a),
        out_shape=fm(a_width),
        scratch_shapes=scratch(dve_a, units_a),
        compiler_params=cparams(dimension_semantics=sem),
        name="attn_a",
    )(qa, qa, ka, ka, va, ga)

    dve_b = 2 * HEAD_DIM + BF16_SUBLANES
    kb_spec = lambda index: pl.BlockSpec((1, seq, STEP_ROWS), index)

    def kb_next(b, u, i):
        b2, u2, _ = next_step(grid_b)(b, u, i)
        return b2, 0, u2

    ob = pl.pallas_call(
        functools.partial(_attn_kernel, mode="B", n_chunks=nc, lam_init=lam_init, grid_dims=grid_b, group=group,
                          units=units_b),
        grid=grid_b,
        in_specs=[step_spec(units_b), next_q_spec(grid_b, units_b),
                  kb_spec(lambda b, u, i: (b, 0, u)),
                  kb_spec(kb_next),
                  pl.BlockSpec((1, nc, 2 * dve_b, t), lambda b, u, i: (b, 0, u, 0)),
                  step_spec(units_b),
                  pl.BlockSpec((4, HEAD_DIM), lambda b, u, i: (0, 0)),
                  pl.BlockSpec((2 * HEAD_DIM, 1), lambda b, u, i: (0, 0))],
        out_specs=step_spec(units_b),
        out_shape=fm(b_width),
        scratch_shapes=scratch(dve_b, units_b),
        compiler_params=cparams(dimension_semantics=sem),
        name="attn_b",
    )(qb, qb, kb, kb, vb, gb, lam_vecs, subln)

    ot = OUT_TILES
    assert nc % ot == 0
    out_fm_spec = lambda rows: pl.BlockSpec((1, ot, rows, t), lambda b, i: (b, i, 0, 0))
    return pl.pallas_call(
        functools.partial(_out_kernel, a_width=a_width, final=final),
        grid=(bsz, nc // ot),
        in_specs=[pl.BlockSpec((1, ot * t, d), lambda b, i: (b, i, 0)), out_fm_spec(a_width), out_fm_spec(b_width),
                  full(w_out.shape), full((1, d))],
        out_specs=pl.BlockSpec((1, ot * t, d), lambda b, i: (b, i, 0)),
        out_shape=jax.ShapeDtypeStruct(x.shape, x.dtype),
        compiler_params=cparams(dimension_semantics=("parallel", "parallel")),
        name="out_proj",
    )(x, oa, ob, w_out, g_final)


def kernel(x_prompt, x_sample, g_norm, w_in, a_q_norm, a_k_norm, b_lambda_q1, b_lambda_k1,
           b_lambda_q2, b_lambda_k2, b_subln, w_out, g_final):
    depth = w_in.shape[0]
    hp, hs = x_prompt, x_sample
    for l in range(depth):
        lam_init = 0.8 - 0.6 * math.exp(-0.3 * l)
        args = (g_norm[l][None, :], w_in[l].T.astype(BF16), a_q_norm[l][:, None], a_k_norm[l][:, None],
                jnp.stack([b_lambda_q1[l], b_lambda_k1[l], b_lambda_q2[l], b_lambda_k2[l]]).astype(F32),
                b_subln[l][:, None], w_out[l].astype(BF16), g_final[None, :])
        final = l == depth - 1
        hp = _layer(hp, *args, lam_init, final)
        hs = _layer(hs, *args, lam_init, final)
    return (hp, hs)
```

```python
import functools
import math

import jax
import jax.numpy as jnp
from jax import lax
from jax.experimental import pallas as pl
from jax.experimental.pallas import tpu as pltpu

HEAD_DIM = 64
GRID_W = 64
ROPE_THETA = 10000.0
EPS = 1e-6
TOKEN_TILE = 512
SUB_KEYS = 256
HEADS_PER_STEP = 4
STEP_ROWS = HEADS_PER_STEP * HEAD_DIM
PHASE_PAIRS = 16
PROJ_TILES = 2
OUT_TILES = 4
BF16_SUBLANES = 16
LANES = 128
KEY_LANES = 256
VMEM_LIMIT_BYTES = 56 * 1024 * 1024
LOG2E = math.log2(math.e)

F32 = jnp.float32
BF16 = jnp.bfloat16


def _dot_nt(a, b):
    return lax.dot_general(a, b, (((1,), (1,)), ((), ())), preferred_element_type=F32)


def _dot_tn(a, b):
    return lax.dot_general(a, b, (((0,), (0,)), ((), ())), preferred_element_type=F32)


def _dot(a, b):
    return jnp.dot(a, b, preferred_element_type=F32)


def _rope_axial(x, c, s):
    xs = jnp.concatenate([x[16:32], x[0:16], x[48:64], x[32:48]], axis=0)
    return x * c + xs * s


def _rope_1d(x, c, s):
    xs = jnp.concatenate([x[32:64], x[0:32]], axis=0)
    return x * c + xs * s


def _silu(g):
    return g * (1.0 / (1.0 + jnp.exp(-g)))


def _proj_kernel(x_ref, gn_ref, wt_ref, qn_ref, kn_ref, ca_ref, sa_ref, cb_ref, sb_ref,
                 qa_ref, va_ref, ga_ref, qb_ref, vb_ref, gb_ref, ka_ref, kb_ref, *, sizes):
    n_tiles, t = qa_ref.shape[1], qa_ref.shape[-1]
    for j in range(n_tiles):
        _proj_tile(j, slice(j * t, (j + 1) * t), x_ref, gn_ref, wt_ref, qn_ref, kn_ref, ca_ref, sa_ref, cb_ref,
                   sb_ref, qa_ref, va_ref, ga_ref, qb_ref, vb_ref, gb_ref, ka_ref, kb_ref, sizes)


def _proj_tile(j, tok, x_ref, gn_ref, wt_ref, qn_ref, kn_ref, ca_ref, sa_ref, cb_ref, sb_ref,
               qa_ref, va_ref, ga_ref, qb_ref, vb_ref, gb_ref, ka_ref, kb_ref, sizes):
    a_width, a_kv, b_qk, b_width = sizes
    x = x_ref[0, tok, :]
    t = x.shape[0]
    h = x * lax.rsqrt(jnp.mean(x * x, axis=-1, keepdims=True) + EPS) * gn_ref[...]
    hb = h.astype(BF16)

    def section(lo, n):
        return _dot_nt(wt_ref[lo:lo + n, :], hb)

    ca, sa, cb, sb = ca_ref[:, tok], sa_ref[:, tok], cb_ref[:, tok], sb_ref[:, tok]
    qscale = HEAD_DIM ** -0.5 * LOG2E
    ones_rows = (lax.broadcasted_iota(jnp.int32, (BF16_SUBLANES, t), 0) == 0).astype(BF16)

    def head_norm(blk, g):
        return blk * lax.rsqrt(jnp.mean(blk * blk, axis=0, keepdims=True) + EPS) * g

    def store_values(v_ref, vals, dv):
        dve = dv + BF16_SUBLANES
        for hd in range(vals.shape[0] // dv):
            v_ref[0, j, hd * dve:hd * dve + dv, :] = vals[hd * dv:(hd + 1) * dv].astype(BF16)
            v_ref[0, j, hd * dve + dv:(hd + 1) * dve, :] = ones_rows

    off = 0
    r = section(off, a_width)
    for hd in range(a_width // HEAD_DIM):
        blk = r[hd * HEAD_DIM:(hd + 1) * HEAD_DIM]
        y = _rope_axial(head_norm(blk, qn_ref[...]), ca, sa) * qscale
        qa_ref[0, j, hd * HEAD_DIM:(hd + 1) * HEAD_DIM, :] = y.astype(BF16)
    off += a_width

    r = section(off, 2 * a_kv)
    ks = []
    for hd in range(a_kv // HEAD_DIM):
        blk = r[hd * HEAD_DIM:(hd + 1) * HEAD_DIM]
        ks.append(_rope_axial(head_norm(blk, kn_ref[...]), ca, sa))
    ka_ref[0, tok, 0:a_kv] = jnp.concatenate(ks, axis=0).T.astype(BF16)
    ka_ref[0, tok, a_kv:] = jnp.zeros((t, ka_ref.shape[-1] - a_kv), BF16)
    store_values(va_ref, r[a_kv:2 * a_kv], HEAD_DIM)
    off += 2 * a_kv

    ga_ref[0, j] = _silu(section(off, a_width)).astype(BF16)
    off += a_width

    r = section(off, b_qk)
    for hd in range(b_qk // HEAD_DIM):
        blk = r[hd * HEAD_DIM:(hd + 1) * HEAD_DIM]
        qb_ref[0, j, hd * HEAD_DIM:(hd + 1) * HEAD_DIM, :] = (_rope_1d(blk, cb, sb) * qscale).astype(BF16)
    off += b_qk

    r = section(off, b_qk)
    ks = [_rope_1d(r[hd * HEAD_DIM:(hd + 1) * HEAD_DIM], cb, sb) for hd in range(b_qk // HEAD_DIM)]
    kb_ref[0, tok, :] = jnp.concatenate(ks, axis=0).T.astype(BF16)
    off += b_qk

    store_values(vb_ref, section(off, b_width), 2 * HEAD_DIM)
    off += b_width
    gb_ref[0, j] = _silu(section(off, b_width)).astype(BF16)


def _attn_kernel(*refs, mode, n_chunks, lam_init, grid_dims, group, units):
    n_s = 2 * group
    if mode == "A":
        q_ref, qn_ref, k_ref, kn_ref, v_ref, g_ref, o_ref = refs[:7]
        dv = HEAD_DIM
    else:
        q_ref, qn_ref, k_ref, kn_ref, v_ref, g_ref, lam_ref, sub_ref, o_ref = refs[:9]
        dv = 2 * HEAD_DIM
    s_refs, (acc_ref, mcar_ref) = refs[-2 - n_s:-2], refs[-2:]
    dve = dv + BF16_SUBLANES
    nq, tq, tk = q_ref.shape[1], q_ref.shape[-1], v_ref.shape[-1]
    n_maps = units * nq * HEADS_PER_STEP
    n_phases = n_maps // group
    assert n_maps % group == 0 and n_phases % 2 == 0
    slots = k_ref.shape[-1] // HEAD_DIM
    n_sub = tk // SUB_KEYS
    _, nu, ni = grid_dims
    b, u, i = pl.program_id(0), pl.program_id(1), pl.program_id(2)
    is_first = (b == 0) & (u == 0) & (i == 0)
    u_next = (u + (i + 1) // ni) % nu
    z = jnp.zeros((HEAD_DIM, tq), BF16)

    def split(m):
        ul, rest = divmod(m, nq * HEADS_PER_STEP)
        tl, hd = divmod(rest, HEADS_PER_STEP)
        return ul, tl, hd

    def place(rows64, slot):
        rows = [z] * slots
        rows[slot] = rows64
        return jnp.concatenate(rows, axis=0)

    def expand(rows64, hd, step_unit, ul):
        if mode == "B":
            return place(rows64, hd)
        if nu == 1:
            return place(rows64, ul)
        return jnp.where(step_unit * units + ul == 0, place(rows64, 0), place(rows64, 1))

    def q_rows(ul, hd):
        return slice(ul * STEP_ROWS + hd * HEAD_DIM, ul * STEP_ROWS + (hd + 1) * HEAD_DIM)

    qx = []
    for m in range(n_maps):
        ul, tl, hd = split(m)
        qx.append(expand(q_ref[0, tl, q_rows(ul, hd), :], hd, u, ul))
    qx_next = [expand(qn_ref[0, 0, q_rows(0, m), :], m, u_next, 0) for m in range(group)]

    def v_rows(m):
        ul, _, hd = split(m)
        blk = ul if mode == "A" else ul * (HEADS_PER_STEP // 2) + hd // 2
        return slice(blk * dve, (blk + 1) * dve)

    def keys(c, j):
        return pl.ds(c * tk + j * SUB_KEYS, SUB_KEYS)

    def pass1(kref, qxm, s_dst, c, j, m8):
        s = _dot(kref[0, keys(c, j), :], qxm)
        s_dst[keys(c, j), :] = s
        return jnp.maximum(m8, jnp.max(s.reshape(SUB_KEYS // 8, 8, tq), axis=0))

    def pass2(m, s_src, mb, c, j):
        s = s_src[keys(c, j), :]
        p = jnp.exp2(s.reshape(SUB_KEYS // 8, 8, tq) - mb[None]).reshape(SUB_KEYS, tq).astype(BF16)
        vs = v_ref[0, c, v_rows(m), j * SUB_KEYS:(j + 1) * SUB_KEYS]
        return _dot(vs, p)

    neg = jnp.full((8, tq), -jnp.inf, F32)

    def normalised(m, lanes):
        return acc_ref[m, 0:dv, lanes] * (1.0 / acc_ref[m, dv:dv + 1, lanes])

    if mode == "B":
        lv = lam_ref[...]
        lam = (jnp.exp(jnp.sum(lv[0:1] * lv[1:2], axis=-1, keepdims=True))
               - jnp.exp(jnp.sum(lv[2:3] * lv[3:4], axis=-1, keepdims=True)) + lam_init)
        sub_gain = sub_ref[...] * (1.0 - lam_init)

    def finish(map_ids):
        for m in map_ids[::dv // HEAD_DIM]:
            ul, t_idx, hd = split(m)
            rows = slice(ul * STEP_ROWS + hd * HEAD_DIM, ul * STEP_ROWS + hd * HEAD_DIM + dv)
            for strip in range(tq // LANES):
                lanes = slice(strip * LANES, (strip + 1) * LANES)
                g = g_ref[0, t_idx, rows, lanes].astype(F32)
                if mode == "A":
                    y = normalised(m, lanes)
                else:
                    o = normalised(m, lanes) - lam * normalised(m + 1, lanes)
                    y = o * lax.rsqrt(jnp.mean(o * o, axis=0, keepdims=True) + EPS) * sub_gain
                o_ref[0, t_idx, rows, lanes] = (y * g).astype(BF16)

    def run_phase(trips, p1s, p2s, done=()):
        def phase(_, m8s):
            m8s = list(m8s)
            for c in range(n_chunks):
                parts = [None] * len(p2s)
                for j in range(n_sub):
                    for k in range(max(len(p1s), len(p2s))):
                        if k < len(p1s):
                            m8s[k] = pass1(*p1s[k], c, j, m8s[k])
                        if k < len(p2s):
                            d = pass2(*p2s[k], c, j)
                            parts[k] = d if parts[k] is None else parts[k] + d
                for k, p2 in enumerate(p2s):
                    if c == 0:
                        acc_ref[p2[0]] = parts[k]
                    else:
                        acc_ref[p2[0]] += parts[k]
            finish(done)
            return tuple(m8s)

        return lax.fori_loop(0, trips, phase, (neg,) * len(p1s))

    def col_max(m8):
        return jnp.broadcast_to(jnp.max(m8, axis=0, keepdims=True), (8, tq))

    def s_buf(parity, k):
        return s_refs[parity * group + k]

    m_first = run_phase(is_first.astype(jnp.int32), [(k_ref, qx[k], s_buf(0, k)) for k in range(group)], [])

    @pl.when(is_first)
    def _():
        for k in range(group):
            mcar_ref[k] = m_first[k]

    once = jnp.minimum(i, 0) + 1
    m8s = [mcar_ref[k] for k in range(group)]
    for ph in range(1, n_phases + 1):
        if ph < n_phases:
            p1s = [(k_ref, qx[ph * group + k], s_buf(ph % 2, k)) for k in range(group)]
        else:
            p1s = [(kn_ref, qx_next[k], s_buf(0, k)) for k in range(group)]
        p2s = [((ph - 1) * group + k, s_buf((ph - 1) % 2, k), col_max(m8s[k])) for k in range(group)]
        done = tuple(range((ph - 2) * group, (ph - 1) * group)) if ph >= 2 else ()
        m8s = run_phase(once, p1s, p2s, done)
    for k in range(group):
        mcar_ref[k] = m8s[k]
    finish(tuple(range((n_phases - 1) * group, n_maps)))


def _out_kernel(x_ref, oa_ref, ob_ref, w_ref, gf_ref, y_ref, *, a_width, final):
    n_tiles, t = oa_ref.shape[1], oa_ref.shape[-1]
    for j in range(n_tiles):
        tok = slice(j * t, (j + 1) * t)
        y = (x_ref[0, tok, :] + _dot_tn(oa_ref[0, j], w_ref[0:a_width, :])
             + _dot_tn(ob_ref[0, j], w_ref[a_width:, :]))
        if final:
            y = y * lax.rsqrt(jnp.mean(y * y, axis=-1, keepdims=True) + EPS) * gf_ref[...]
        y_ref[0, tok, :] = y


def _rope_tables(seq):
    def cs(pos, dim):
        inv = ROPE_THETA ** (-jnp.arange(0, dim, 2, dtype=F32) / dim)
        ang = pos[None, :] * inv[:, None]
        return jnp.cos(ang), jnp.sin(ang)

    rows = seq // GRID_W
    row = jnp.repeat(jnp.arange(rows, dtype=F32), GRID_W)
    col = jnp.tile(jnp.arange(GRID_W, dtype=F32), rows)
    pos = jnp.arange(seq, dtype=F32)
    cr, sr = cs(row, HEAD_DIM // 2)
    cc, sc = cs(col, HEAD_DIM // 2)
    cp, sp = cs(pos, HEAD_DIM)
    ca = jnp.concatenate([cr, cr, cc, cc], axis=0)
    sa = jnp.concatenate([-sr, sr, -sc, sc], axis=0)
    cb = jnp.concatenate([cp, cp], axis=0)
    sb = jnp.concatenate([-sp, sp], axis=0)
    return ca, sa, cb, sb


def _layer(x, gn, wt, qn, kn, lam_vecs, subln, w_out, g_final, lam_init, final):
    bsz, seq, d = x.shape
    a_width = (d // (2 * HEAD_DIM)) * HEAD_DIM
    a_kv = max(1, (a_width // HEAD_DIM) // 4) * HEAD_DIM
    b_qk = (d // (4 * HEAD_DIM)) * 2 * HEAD_DIM
    b_width = b_qk
    assert a_width == HEADS_PER_STEP * a_kv and a_width % STEP_ROWS == 0 and b_width % STEP_ROWS == 0
    t = TOKEN_TILE
    assert seq % t == 0 and seq % GRID_W == 0
    nc = seq // t
    va_rows = (a_kv // HEAD_DIM) * (HEAD_DIM + BF16_SUBLANES)
    vb_rows = (b_width // (2 * HEAD_DIM)) * (2 * HEAD_DIM + BF16_SUBLANES)
    ca, sa, cb, sb = _rope_tables(seq)
    cparams = functools.partial(pltpu.CompilerParams, vmem_limit_bytes=VMEM_LIMIT_BYTES)

    def fm(rows):
        return jax.ShapeDtypeStruct((bsz, nc, rows, t), BF16)

    pt = PROJ_TILES
    assert nc % pt == 0
    fm_spec = lambda rows: pl.BlockSpec((1, pt, rows, t), lambda b, i: (b, i, 0, 0))
    tab_spec = pl.BlockSpec((HEAD_DIM, pt * t), lambda b, i: (0, i))
    full = lambda shape: pl.BlockSpec(shape, lambda b, i: (0,) * len(shape))

    qa, va, ga, qb, vb, gb, ka, kb = pl.pallas_call(
        functools.partial(_proj_kernel, sizes=(a_width, a_kv, b_qk, b_width)),
        grid=(bsz, nc // pt),
        in_specs=[pl.BlockSpec((1, pt * t, d), lambda b, i: (b, i, 0)), full((1, d)), full(wt.shape),
                  full((HEAD_DIM, 1)), full((HEAD_DIM, 1)), tab_spec, tab_spec, tab_spec, tab_spec],
        out_specs=[fm_spec(a_width), fm_spec(va_rows), fm_spec(a_width), fm_spec(b_qk), fm_spec(vb_rows),
                   fm_spec(b_width),
                   pl.BlockSpec((1, pt * t, KEY_LANES), lambda b, i: (b, i, 0)),
                   pl.BlockSpec((1, pt * t, b_qk), lambda b, i: (b, i, 0))],
        out_shape=[fm(a_width), fm(va_rows), fm(a_width), fm(b_qk), fm(vb_rows), fm(b_width),
                   jax.ShapeDtypeStruct((bsz, seq, KEY_LANES), BF16),
                   jax.ShapeDtypeStruct((bsz, seq, b_qk), BF16)],
        compiler_params=cparams(dimension_semantics=("parallel", "parallel")),
        name="proj",
    )(x, gn, wt, qn, kn, ca, sa, cb, sb)

    group = max(1, min(HEADS_PER_STEP, PHASE_PAIRS // nc))
    nq = max(1, 2 * group // HEADS_PER_STEP)
    assert nc % nq == 0
    units_a = a_width // STEP_ROWS
    units_b = 1
    grid_a = (bsz, a_width // STEP_ROWS // units_a, nc // nq)
    grid_b = (bsz, b_width // STEP_ROWS // units_b, nc // nq)

    def next_step(dims):
        nb, nu, ni = dims

        def f(b, u, i):
            cu = (i + 1) // ni
            cb = (u + cu) // nu
            return jnp.minimum(b + cb, nb - 1), (u + cu) % nu, (i + 1) % ni
        return f

    step_spec = lambda units: pl.BlockSpec((1, nq, units * STEP_ROWS, t), lambda b, u, i: (b, i, u, 0))
    sem = ("arbitrary", "arbitrary", "arbitrary")

    def next_q_spec(dims, units):
        nxt = next_step(dims)
        rows = group * HEAD_DIM

        def index(b, u, i):
            b2, u2, i2 = nxt(b, u, i)
            return b2, i2 * nq, u2 * (units * STEP_ROWS // rows), 0
        return pl.BlockSpec((1, 1, rows, t), index)

    def scratch(dve, units):
        return ([pltpu.VMEM((seq, t), F32)] * (2 * group)
                + [pltpu.VMEM((units * nq * HEADS_PER_STEP, dve, t), F32), pltpu.VMEM((group, 8, t), F32)])

    dve_a = HEAD_DIM + BF16_SUBLANES
    ka_spec = lambda index: pl.BlockSpec((1, seq, KEY_LANES), index)
    oa = pl.pallas_call(
        functools.partial(_attn_kernel, mode="A", n_chunks=nc, lam_init=lam_init, grid_dims=grid_a, group=group,
                          units=units_a),
        grid=grid_a,
        in_specs=[step_spec(units_a), next_q_spec(grid_a, units_a),
                  ka_spec(lambda b, u, i: (b, 0, 0)),
                  ka_spec(lambda b, u, i: (next_step(grid_a)(b, u, i)[0], 0, 0)),
                  pl.BlockSpec((1, nc, units_a * dve_a, t), lambda b, u, i: (b, 0, u, 0)),
                  step_spec(units_a)],
        out_specs=step_spec(units_a),
        out_shape=fm(a_width),
        scratch_shapes=scratch(dve_a, units_a),
        compiler_params=cparams(dimension_semantics=sem),
        name="attn_a",
    )(qa, qa, ka, ka, va, ga)

    dve_b = 2 * HEAD_DIM + BF16_SUBLANES
    kb_spec = lambda index: pl.BlockSpec((1, seq, STEP_ROWS), index)

    def kb_next(b, u, i):
        b2, u2, _ = next_step(grid_b)(b, u, i)
        return b2, 0, u2

    ob = pl.pallas_call(
        functools.partial(_attn_kernel, mode="B", n_chunks=nc, lam_init=lam_init, grid_dims=grid_b, group=group,
                          units=units_b),
        grid=grid_b,
        in_specs=[step_spec(units_b), next_q_spec(grid_b, units_b),
                  kb_spec(lambda b, u, i: (b, 0, u)),
                  kb_spec(kb_next),
                  pl.BlockSpec((1, nc, 2 * dve_b, t), lambda b, u, i: (b, 0, u, 0)),
                  step_spec(units_b),
                  pl.BlockSpec((4, HEAD_DIM), lambda b, u, i: (0, 0)),
                  pl.BlockSpec((2 * HEAD_DIM, 1), lambda b, u, i: (0, 0))],
        out_specs=step_spec(units_b),
        out_shape=fm(b_width),
        scratch_shapes=scratch(dve_b, units_b),
        compiler_params=cparams(dimension_semantics=sem),
        name="attn_b",
    )(qb, qb, kb, kb, vb, gb, lam_vecs, subln)

    ot = OUT_TILES
    assert nc % ot == 0
    out_fm_spec = lambda rows: pl.BlockSpec((1, ot, rows, t), lambda b, i: (b, i, 0, 0))
    return pl.pallas_call(
        functools.partial(_out_kernel, a_width=a_width, final=final),
        grid=(bsz, nc // ot),
        in_specs=[pl.BlockSpec((1, ot * t, d), lambda b, i: (b, i, 0)), out_fm_spec(a_width), out_fm_spec(b_width),
                  full(w_out.shape), full((1, d))],
        out_specs=pl.BlockSpec((1, ot * t, d), lambda b, i: (b, i, 0)),
        out_shape=jax.ShapeDtypeStruct(x.shape, x.dtype),
        compiler_params=cparams(dimension_semantics=("parallel", "parallel")),
        name="out_proj",
    )(x, oa, ob, w_out, g_final)


def kernel(x_prompt, x_sample, g_norm, w_in, a_q_norm, a_k_norm, b_lambda_q1, b_lambda_k1,
           b_lambda_q2, b_lambda_k2, b_subln, w_out, g_final):
    depth = w_in.shape[0]
    hp, hs = x_prompt, x_sample
    for l in range(depth):
        lam_init = 0.8 - 0.6 * math.exp(-0.3 * l)
        args = (g_norm[l][None, :], w_in[l].T.astype(BF16), a_q_norm[l][:, None], a_k_norm[l][:, None],
                jnp.stack([b_lambda_q1[l], b_lambda_k1[l], b_lambda_q2[l], b_lambda_k2[l]]).astype(F32),
                b_subln[l][:, None], w_out[l].astype(BF16), g_final[None, :])
        final = l == depth - 1
        hp = _layer(hp, *args, lam_init, final)
        hs = _layer(hs, *args, lam_init, final)
    return (hp, hs)
```
